```python
import math
import jax
import jax.numpy as jnp
from jax import lax
import numpy as np

D_MODEL = 1024
BATCH = 16
SEQ = 4096
DEPTH = 4

GRID_W = 64
CTX_LEN = 256
N_MIXERS = 3
N_RET = len(range(0, DEPTH, N_MIXERS))
N_HG = len(range(1, DEPTH, N_MIXERS))
N_M2 = len(range(2, DEPTH, N_MIXERS))
NORM_EPS = 1e-6

RET_HEADS = 4
RET_DK = D_MODEL // RET_HEADS
RET_DV = 2 * RET_DK
RET_PROJ = 2 * RET_HEADS * RET_DK + 2 * RET_HEADS * RET_DV
RET_CHUNK = 64
ROPE_BASE = 10000.0

HG_DK = 128
HG_HEADS = D_MODEL // HG_DK
HG_PROJ = 5 * D_MODEL
HG_CHUNK = 32

M2_DINNER = 2 * D_MODEL
M2_HEADDIM = 64
M2_HEADS = M2_DINNER // M2_HEADDIM
M2_GROUPS = 4
M2_HPG = M2_HEADS // M2_GROUPS
M2_DSTATE = 128
M2_CONV_DIM = M2_DINNER + 2 * M2_GROUPS * M2_DSTATE
M2_PROJ = M2_DINNER + M2_CONV_DIM + 2 * M2_HEADS
M2_CONV_W = 3
M2_CHUNK = 64

FFN_HIDDEN = 2816
FFN_CONV_W = 3

kernel_name = 'hybrid_ret_hgrn2_ssd_prefix_dit'


def rms_norm(x, g):
    xf = x.astype(jnp.float32)
    y = xf * lax.rsqrt(jnp.mean(xf * xf, -1, keepdims=True) + NORM_EPS)
    return (y * g.astype(jnp.float32)).astype(x.dtype)


def dwconv(x, w, b):
    width, ch = w.shape
    y = lax.conv_general_dilated(x, w[:, None, :].astype(x.dtype), window_strides=(1,),
                                 padding=[(width // 2, width // 2)],
                                 dimension_numbers=('NWC', 'WIO', 'NWC'), feature_group_count=ch)
    return y + b.astype(x.dtype)


def row_conv(x, w, b):
    bt, length, ch = x.shape
    rows = length // GRID_W
    return dwconv(x.reshape(bt * rows, GRID_W, ch), w, b).reshape(bt, length, ch)


def axial_rope(t):
    length, dk = t.shape[2], t.shape[-1]
    nf = dk // 4
    pos = jnp.arange(length)
    row = (pos // GRID_W).astype(jnp.float32)
    col = (pos % GRID_W).astype(jnp.float32)
    inv = ROPE_BASE ** (-jnp.arange(nf, dtype=jnp.float32) / nf)

    def rot(u, p):
        ang = p[:, None] * inv
        cos, sin = jnp.cos(ang), jnp.sin(ang)
        u1, u2 = u[..., :nf], u[..., nf:]
        return jnp.concatenate([u1 * cos - u2 * sin, u1 * sin + u2 * cos], -1)

    tf = t.astype(jnp.float32)
    return jnp.concatenate([rot(tf[..., :2 * nf], row), rot(tf[..., 2 * nf:], col)], -1).astype(t.dtype)


def chunk_scan(q, k, v, log_a, s0, chunk, return_out):
    f32 = jnp.float32
    bt, g, length, kd = q.shape
    r, vd = v.shape[2], v.shape[-1]
    n = length // chunk
    per_channel = log_a.shape[-1] != 1

    def blocks(t, axis):
        t = t.astype(f32).reshape(t.shape[:axis] + (n, chunk) + t.shape[axis + 1:])
        return jnp.moveaxis(t, axis, 0)

    tril = jnp.tril(jnp.ones((chunk, chunk), dtype=bool))[:, :, None]

    def step(state, inp):
        qc, kc, vc, ac = inp
        b = jnp.cumsum(ac, axis=3)
        b_last = b[:, :, :, -1:, :]
        kw = kc[:, :, None] * jnp.exp(b_last - b)
        new_state = state * jnp.exp(b_last)[:, :, :, 0, :, None] + jnp.einsum('bgrsk,bgrsv->bgrkv', kw, vc)
        if not return_out:
            return new_state, None
        o = jnp.einsum('bgrtk,bgrkv->bgrtv', qc[:, :, None] * jnp.exp(b), state)
        diff = b[:, :, :, :, None, :] - b[:, :, :, None, :, :]
        dec = jnp.exp(jnp.where(tril, diff, -jnp.inf))
        if per_channel:
            scores = jnp.einsum('bgtk,bgrtsk,bgsk->bgrts', qc, dec, kc)
        else:
            scores = jnp.einsum('bgtk,bgsk->bgts', qc, kc)[:, :, None] * dec[..., 0]
        o = o + jnp.einsum('bgrts,bgrsv->bgrtv', scores, vc)
        return new_state, o

    state, o = lax.scan(step, s0, (blocks(q, 2), blocks(k, 2), blocks(v, 3), blocks(log_a, 3)))
    if return_out:
        o = jnp.moveaxis(o, 0, 3).reshape(bt, g, r, length, vd)
    return o, state


def bidir_scan(q_c, k_c, v_c, a_c, q_l, k_l, v_l, a_l, chunk, need_ctx):
    bt, g, _, kd = q_l.shape
    r, vd = v_l[0].shape[2], v_l[0].shape[-1]
    outs_c, outs_l = [], []
    for d in range(2):
        rev = (lambda t, ax: jnp.flip(t, ax)) if d == 1 else (lambda t, ax: t)
        s0 = jnp.zeros((bt, g, r, kd, vd), jnp.float32)
        o_c, s_c = chunk_scan(rev(q_c, 2), rev(k_c[d], 2), rev(v_c[d], 3), rev(a_c[d], 3), s0, chunk, need_ctx)
        o_l, _ = chunk_scan(rev(q_l, 2), rev(k_l[d], 2), rev(v_l[d], 3), rev(a_l[d], 3), s_c, chunk, True)
        outs_l.append(rev(o_l, 3))
        if need_ctx:
            outs_c.append(rev(o_c, 3))
    y_c = outs_c[0] + outs_c[1] if need_ctx else None
    return y_c, outs_l[0] + outs_l[1]


def retention_mixer(u_c, u_l, w_in, w_out, decay, gn, need_ctx):
    hk, hv = RET_HEADS * RET_DK, RET_HEADS * RET_DV
    log_gamma = jax.nn.log_sigmoid(decay.astype(jnp.float32))

    def project(u, grid):
        bt, length, _ = u.shape
        q, k, v, g = jnp.split(u @ w_in, [hk, 2 * hk, 2 * hk + hv], axis=-1)
        q = q.reshape(bt, length, RET_HEADS, RET_DK).transpose(0, 2, 1, 3)
        k = k.reshape(bt, length, RET_HEADS, RET_DK).transpose(0, 2, 1, 3) * (RET_DK ** -0.5)
        if grid:
            q, k = axial_rope(q), axial_rope(k)
        v = v.reshape(bt, length, RET_HEADS, RET_DV).transpose(0, 2, 1, 3)[:, :, None]
        a = tuple(jnp.broadcast_to(log_gamma[d][None, :, None, None, None], (bt, RET_HEADS, 1, length, 1))
                  for d in range(2))
        return q, (k, k), (v, v), a, g

    def readout(o, g):
        bt, _, _, length, _ = o.shape
        o = o[:, :, 0].transpose(0, 2, 1, 3)
        o = o - o.mean(-1, keepdims=True)
        o = o * lax.rsqrt(jnp.mean(o * o, -1, keepdims=True) + NORM_EPS)
        o = o.reshape(bt, length, hv) * gn.astype(jnp.float32)
        return (jax.nn.silu(g.astype(jnp.float32)) * o).astype(w_out.dtype) @ w_out

    q_c, k_c, v_c, a_c, g_c = project(u_c, False)
    q_l, k_l, v_l, a_l, g_l = project(u_l, True)
    o_c, o_l = bidir_scan(q_c, k_c, v_c, a_c, q_l, k_l, v_l, a_l, RET_CHUNK, need_ctx)
    y_c = readout(o_c, g_c) if need_ctx else None
    return y_c, readout(o_l, g_l)


def hgrn2_mixer(u_c, u_l, w_in, w_out, lb, gn, need_ctx):
    lbh = lb.astype(jnp.float32).reshape(1, HG_HEADS, 1, HG_DK)

    def project(u):
        bt, length, _ = u.shape
        heads = lambda t: t.astype(jnp.float32).reshape(bt, length, HG_HEADS, HG_DK).transpose(0, 2, 1, 3)
        q, f_f, f_b, i, g = jnp.split(u @ w_in, 5, axis=-1)
        ks, las = [], []
        for f in (f_f, f_b):
            f = heads(f)
            ks.append((1.0 - lbh) * jax.nn.sigmoid(-f))
            las.append(jnp.logaddexp(jnp.log(lbh), jnp.log1p(-lbh) + jax.nn.log_sigmoid(f))[:, :, None])
        v = heads(i)[:, :, None]
        return jax.nn.silu(heads(q)), tuple(ks), (v, v), tuple(las), g

    def readout(o, g):
        bt, _, _, length, _ = o.shape
        o = o[:, :, 0].transpose(0, 2, 1, 3)
        o = o * lax.rsqrt(jnp.mean(o * o, -1, keepdims=True) + NORM_EPS)
        o = o.reshape(bt, length, D_MODEL) * gn.astype(jnp.float32)
        return (jax.nn.silu(g.astype(jnp.float32)) * o).astype(w_out.dtype) @ w_out

    q_c, k_c, v_c, a_c, g_c = project(u_c)
    q_l, k_l, v_l, a_l, g_l = project(u_l)
    o_c, o_l = bidir_scan(q_c, k_c, v_c, a_c, q_l, k_l, v_l, a_l, HG_CHUNK, need_ctx)
    y_c = readout(o_c, g_c) if need_ctx else None
    return y_c, readout(o_l, g_l)


def mamba2_mixer(u_c, u_l, w_in, w_out, conv_w, conv_b, dt_bias, a_log, d_skip, gn, need_ctx):
    f32 = jnp.float32
    a_neg = -jnp.exp(a_log.astype(f32))
    gsz = M2_DINNER // M2_GROUPS

    def project(u):
        bt, length, _ = u.shape
        z, xbc, dt = jnp.split(u @ w_in, [M2_DINNER, M2_DINNER + M2_CONV_DIM], axis=-1)
        xbc = jax.nn.silu(dwconv(xbc, conv_w, conv_b))
        xs, bm, cm = jnp.split(xbc, [M2_DINNER, M2_DINNER + M2_GROUPS * M2_DSTATE], axis=-1)
        xh = xs.astype(f32).reshape(bt, length, M2_GROUPS, M2_HPG, M2_HEADDIM).transpose(0, 2, 3, 1, 4)
        kb = bm.reshape(bt, length, M2_GROUPS, M2_DSTATE).transpose(0, 2, 1, 3)
        qc = cm.reshape(bt, length, M2_GROUPS, M2_DSTATE).transpose(0, 2, 1, 3)
        dt = jax.nn.softplus(dt.astype(f32).reshape(bt, length, 2, M2_HEADS) + dt_bias.astype(f32))
        dt = dt.reshape(bt, length, 2, M2_GROUPS, M2_HPG).transpose(2, 0, 3, 4, 1)[..., None]
        la = dt * a_neg.reshape(2, 1, M2_GROUPS, M2_HPG, 1, 1)
        return qc, (kb, kb), (xh * dt[0], xh * dt[1]), (la[0], la[1]), xh, z

    def readout(o, xh, z):
        bt, _, _, length, _ = o.shape
        y = o + d_skip.astype(f32).reshape(1, M2_GROUPS, M2_HPG, 1, 1) * xh
        y = y.transpose(0, 3, 1, 2, 4).reshape(bt, length, M2_GROUPS, gsz)
        y = y * jax.nn.silu(z.astype(f32).reshape(bt, length, M2_GROUPS, gsz))
        y = y * lax.rsqrt(jnp.mean(y * y, -1, keepdims=True) + NORM_EPS)
        y = y.reshape(bt, length, M2_DINNER) * gn.astype(f32)
        return y.astype(w_out.dtype) @ w_out

    q_c, k_c, v_c, a_c, x_c, z_c = project(u_c)
    q_l, k_l, v_l, a_l, x_l, z_l = project(u_l)
    o_c, o_l = bidir_scan(q_c, k_c, v_c, a_c, q_l, k_l, v_l, a_l, M2_CHUNK, need_ctx)
    y_c = readout(o_c, x_c, z_c) if need_ctx else None
    return y_c, readout(o_l, x_l, z_l)


def conv_ffn(u, w_up, conv_w, conv_b, w_down, grid):
    a, v = jnp.split(u @ w_up, 2, axis=-1)
    a = row_conv(a, conv_w, conv_b) if grid else dwconv(a, conv_w, conv_b)
    return (jax.nn.gelu(a) * v) @ w_down


def setup_inputs(seed: int = 0) -> dict:
    key = jax.random.key(seed)
    keys = iter(jax.random.split(key, 32))
    f32 = jnp.float32

    def nrm(shape, scale):
        return jax.random.normal(next(keys), shape, f32) * scale

    D = D_MODEL
    x = nrm((BATCH, SEQ, D), 1.0)
    c = nrm((BATCH, D), 1.0)
    ctx = nrm((BATCH, CTX_LEN, D), 1.0)
    c_ctx = nrm((D,), 1.0)
    ada_w = nrm((DEPTH, D, 6 * D), 0.5 * D ** -0.5)
    ada_b = nrm((DEPTH, 6 * D), 0.02)
    norm_g = 1.0 + nrm((DEPTH, 4, D), 0.05)
    ret_w_in = nrm((N_RET, D, RET_PROJ), D ** -0.5)
    ret_w_out = nrm((N_RET, RET_HEADS * RET_DV, D), (RET_HEADS * RET_DV) ** -0.5)
    eps = 2.0 ** (-5.0 - jnp.arange(RET_HEADS, dtype=f32))
    ret_decay = jnp.log((1.0 - eps) / eps) + nrm((N_RET, 2, RET_HEADS), 0.1)
    ret_gn = 1.0 + nrm((N_RET, RET_HEADS * RET_DV), 0.05)
    hg_w_in = nrm((N_HG, D, HG_PROJ), D ** -0.5)
    hg_w_out = nrm((N_HG, D, D), D ** -0.5)
    hg_lb = nrm((DEPTH, HG_HEADS * HG_DK), 0.1)
    hg_gn = 1.0 + nrm((N_HG, D), 0.05)
    m2_w_in = nrm((N_M2, D, M2_PROJ), D ** -0.5)
    m2_w_out = nrm((N_M2, M2_DINNER, D), M2_DINNER ** -0.5)
    m2_conv_w = nrm((N_M2, M2_CONV_W, M2_CONV_DIM), M2_CONV_W ** -0.5)
    m2_conv_b = nrm((N_M2, M2_CONV_DIM), 0.02)
    dt = jnp.exp(jax.random.uniform(next(keys), (N_M2, 2, M2_HEADS), f32, math.log(1e-3), math.log(1e-1)))
    m2_dt_bias = dt + jnp.log(-jnp.expm1(-dt))
    m2_a_log = jnp.log(jax.random.uniform(next(keys), (N_M2, 2, M2_HEADS), f32, 1.0, 16.0))
    m2_d = 1.0 + nrm((N_M2, M2_HEADS), 0.1)
    m2_gn = 1.0 + nrm((N_M2, M2_DINNER), 0.05)
    ffn_w_up = nrm((DEPTH, D, 2 * FFN_HIDDEN), D ** -0.5)
    ffn_conv_w = nrm((DEPTH, FFN_CONV_W, FFN_HIDDEN), FFN_CONV_W ** -0.5)
    ffn_conv_b = nrm((DEPTH, FFN_HIDDEN), 0.02)
    ffn_w_down = nrm((DEPTH, FFN_HIDDEN, D), FFN_HIDDEN ** -0.5)
    return {'x': x, 'c': c, 'ctx': ctx, 'c_ctx': c_ctx, 'ada_w': ada_w, 'ada_b': ada_b, 'norm_g': norm_g,
            'ret_w_in': ret_w_in, 'ret_w_out': ret_w_out, 'ret_decay': ret_decay, 'ret_gn': ret_gn,
            'hg_w_in': hg_w_in, 'hg_w_out': hg_w_out, 'hg_lb': hg_lb, 'hg_gn': hg_gn,
            'm2_w_in': m2_w_in, 'm2_w_out': m2_w_out, 'm2_conv_w': m2_conv_w, 'm2_conv_b': m2_conv_b,
            'm2_dt_bias': m2_dt_bias, 'm2_a_log': m2_a_log, 'm2_d': m2_d, 'm2_gn': m2_gn,
            'ffn_w_up': ffn_w_up, 'ffn_conv_w': ffn_conv_w, 'ffn_conv_b': ffn_conv_b, 'ffn_w_down': ffn_w_down}


def reference(x, c, ctx, c_ctx, ada_w, ada_b, norm_g, ret_w_in, ret_w_out, ret_decay, ret_gn,
              hg_w_in, hg_w_out, hg_lb, hg_gn, m2_w_in, m2_w_out, m2_conv_w, m2_conv_b,
              m2_dt_bias, m2_a_log, m2_d, m2_gn, ffn_w_up, ffn_conv_w, ffn_conv_b, ffn_w_down):
    h_l, h_c = x, ctx
    lb_cum = jnp.cumsum(jax.nn.softmax(hg_lb.astype(jnp.float32), axis=0), axis=0)
    lb_all = lb_cum - lb_cum[0]
    for i in range(DEPTH):
        kind, j = i % N_MIXERS, i // N_MIXERS
        need_ctx = i < DEPTH - 1
        mod_l = (jax.nn.silu(c) @ ada_w[i] + ada_b[i])[:, None, :]
        mod_c = jax.nn.silu(c_ctx) @ ada_w[i] + ada_b[i]
        sh1_l, sc1_l, g1_l, sh2_l, sc2_l, g2_l = jnp.split(mod_l, 6, axis=-1)
        sh1_c, sc1_c, g1_c, sh2_c, sc2_c, g2_c = jnp.split(mod_c, 6, axis=-1)
        u_l = rms_norm(h_l, norm_g[i, 0]) * (1.0 + sc1_l) + sh1_l
        u_c = rms_norm(h_c, norm_g[i, 0]) * (1.0 + sc1_c) + sh1_c
        if kind == 0:
            y_c, y_l = retention_mixer(u_c, u_l, ret_w_in[j], ret_w_out[j], ret_decay[j], ret_gn[j], need_ctx)
        elif kind == 1:
            y_c, y_l = hgrn2_mixer(u_c, u_l, hg_w_in[j], hg_w_out[j], lb_all[i], hg_gn[j], need_ctx)
        else:
            y_c, y_l = mamba2_mixer(u_c, u_l, m2_w_in[j], m2_w_out[j], m2_conv_w[j], m2_conv_b[j],
                                    m2_dt_bias[j], m2_a_log[j], m2_d[j], m2_gn[j], need_ctx)
        h_l = h_l + g1_l * rms_norm(y_l, norm_g[i, 1])
        u_l = rms_norm(h_l, norm_g[i, 2]) * (1.0 + sc2_l) + sh2_l
        h_l = h_l + g2_l * rms_norm(conv_ffn(u_l, ffn_w_up[i], ffn_conv_w[i], ffn_conv_b[i], ffn_w_down[i], True),
                                    norm_g[i, 3])
        if need_ctx:
            h_c = h_c + g1_c * rms_norm(y_c, norm_g[i, 1])
            u_c = rms_norm(h_c, norm_g[i, 2]) * (1.0 + sc2_c) + sh2_c
            h_c = h_c + g2_c * rms_norm(conv_ffn(u_c, ffn_w_up[i], ffn_conv_w[i], ffn_conv_b[i], ffn_w_down[i], False),
                                        norm_g[i, 3])
    return h_l
```

```python
import functools
import math

import jax
import jax.numpy as jnp
from jax import lax
from jax.experimental import pallas as pl
from jax.experimental.pallas import tpu as pltpu

F32 = jnp.float32
BF16 = jnp.bfloat16

NORM_EPS = 1e-6
GRID_W = 64
ROPE_BASE = 10000.0
N_MIXERS = 3

RET_HEADS = 4
RET_CHUNK = 256
HG_DK = 128
HG_CHUNK = 64
M2_HEADDIM = 64
M2_GROUPS = 4
M2_DSTATE = 128
M2_CHUNK = 128
LANES = 128

V7X_VMEM_LIMIT_BYTES = 56 * 1024 * 1024

_HI = lax.Precision.HIGHEST


def _cparams(*sem):
    return pltpu.CompilerParams(dimension_semantics=sem, vmem_limit_bytes=V7X_VMEM_LIMIT_BYTES)


def _sigmoid(x):
    return 1.0 / (1.0 + jnp.exp(-x))


def _silu(x):
    return x * _sigmoid(x)


def _rms(t):
    return t * lax.rsqrt(jnp.mean(t * t, axis=-1, keepdims=True) + NORM_EPS)


def _dot(a, b):
    return jnp.dot(a, b, preferred_element_type=F32)


def _dot_nt(a, b):
    return lax.dot_general(a, b, (((1,), (1,)), ((), ())), preferred_element_type=F32)


def _dot_tn(a, b):
    return lax.dot_general(a, b, (((0,), (0,)), ((), ())), preferred_element_type=F32)


def _resident(shape):
    return pl.BlockSpec(shape, lambda *_: (0,) * len(shape), pipeline_mode=pl.Buffered(1))


def _mod_kernel(c_ref, w_ref, b_ref, o_ref):
    s = _silu(c_ref[...]).astype(BF16)
    o_ref[0] = _dot(s, w_ref[0].astype(BF16)) + b_ref[0]


def _modulation(cc, ada_w, ada_b):
    depth, d, n = ada_w.shape
    rows = cc.shape[0]
    tn = 1536
    return pl.pallas_call(
        _mod_kernel,
        grid=(depth, n // tn),
        in_specs=[pl.BlockSpec((rows, d), lambda i, j: (0, 0)),
                  pl.BlockSpec((1, d, tn), lambda i, j: (i, 0, j)),
                  pl.BlockSpec((1, 1, tn), lambda i, j: (i, 0, j))],
        out_specs=pl.BlockSpec((1, rows, tn), lambda i, j: (i, 0, j)),
        out_shape=jax.ShapeDtypeStruct((depth, rows, n), F32),
        compiler_params=_cparams("parallel", "parallel"),
        name="adaln_mod",
    )(cc, ada_w, ada_b.reshape(depth, 1, n))


def _norm_mod(x, g, sc, sh):
    return _rms(x) * (g * (1.0 + sc)) + sh


def _proj_kernel(x_ref, g_ref, sc_ref, sh_ref, w_ref, *rest, ncol):
    u = _norm_mod(x_ref[0], g_ref[...], sc_ref[0], sh_ref[0]).astype(BF16)
    if len(rest) == 3:
        w2_ref, o_ref, o2_ref = rest
        o2_ref[0] = _dot(u, w2_ref[...])
    else:
        (o_ref,) = rest
    n = w_ref.shape[1]
    for n0 in range(0, n, ncol):
        n1 = min(n0 + ncol, n)
        o_ref[0, :, n0:n1] = _dot(u, w_ref[:, n0:n1]).astype(o_ref.dtype)


def _project(h, g, sc, sh, w, w2=None, *, tm):
    bt, length, d = h.shape
    n = w.shape[1]
    per_batch = sc.shape[0] != 1
    mod_spec = pl.BlockSpec((1, 1, d), (lambda b, i: (b, 0, 0)) if per_batch else (lambda b, i: (0, 0, 0)))
    in_specs = [pl.BlockSpec((1, tm, d), lambda b, i: (b, i, 0)),
                pl.BlockSpec((1, d), lambda b, i: (0, 0)),
                mod_spec, mod_spec, _resident((d, n))]
    out_specs = [pl.BlockSpec((1, tm, n), lambda b, i: (b, i, 0))]
    out_shape = [jax.ShapeDtypeStruct((bt, length, n), BF16)]
    args = [h, g.reshape(1, d), sc, sh, w]
    if w2 is not None:
        n2 = w2.shape[1]
        in_specs.append(_resident((d, n2)))
        out_specs.append(pl.BlockSpec((1, tm, n2), lambda b, i: (b, i, 0)))
        out_shape.append(jax.ShapeDtypeStruct((bt, length, n2), F32))
        args.append(w2)
    res = pl.pallas_call(
        functools.partial(_proj_kernel, ncol=512),
        grid=(bt, length // tm),
        in_specs=in_specs, out_specs=out_specs, out_shape=out_shape,
        compiler_params=_cparams("parallel", "parallel"),
        name="norm_mod_proj",
    )(*args)
    return res if w2 is not None else res[0]


def _out_kernel(y_ref, w_ref, h_ref, ng_ref, gate_ref, o_ref):
    t = _dot(y_ref[0], w_ref[...])
    o_ref[0] = h_ref[0] + gate_ref[0] * (_rms(t) * ng_ref[...])


def _out_project(y, w, h, ng, gate, *, tm):
    bt, length, d = h.shape
    dy = y.shape[-1]
    per_batch = gate.shape[0] != 1
    gate_spec = pl.BlockSpec((1, 1, d), (lambda b, i: (b, 0, 0)) if per_batch else (lambda b, i: (0, 0, 0)))
    return pl.pallas_call(
        _out_kernel,
        grid=(bt, length // tm),
        in_specs=[pl.BlockSpec((1, tm, dy), lambda b, i: (b, i, 0)),
                  _resident((dy, d)),
                  pl.BlockSpec((1, tm, d), lambda b, i: (b, i, 0)),
                  pl.BlockSpec((1, d), lambda b, i: (0, 0)),
                  gate_spec],
        out_specs=pl.BlockSpec((1, tm, d), lambda b, i: (b, i, 0)),
        out_shape=jax.ShapeDtypeStruct((bt, length, d), F32),
        compiler_params=_cparams("parallel", "parallel"),
        name="out_proj_residual",
    )(y, w, h, ng.reshape(1, d), gate)


def _gelu_tanh(x):
    return 0.5 * x * (1.0 + jnp.tanh(math.sqrt(2.0 / math.pi) * (x + 0.044715 * (x * x * x))))


def _ffn_kernel(h_ref, g_ref, sc_ref, sh_ref, wa_ref, wv_ref, cw_ref, cb_ref, wd_ref, ng_ref, gate_ref, o_ref,
                *, seg, hidden_chunks):
    h = h_ref[0]
    tm = h.shape[0]
    u = _norm_mod(h, g_ref[...], sc_ref[0], sh_ref[0]).astype(BF16)
    pos = lax.broadcasted_iota(jnp.int32, (tm, 1), 0) % seg
    has_prev = pos != 0
    has_next = pos != seg - 1
    acc = None
    for c0, c1 in hidden_chunks:
        a = _dot(u, wa_ref[:, c0:c1])
        v = _dot(u, wv_ref[:, c0:c1])
        a_prev = jnp.where(has_prev, pltpu.roll(a, 1, 0), 0.0)
        a_next = jnp.where(has_next, pltpu.roll(a, tm - 1, 0), 0.0)
        ac = (cw_ref[0:1, c0:c1] * a_prev + cw_ref[1:2, c0:c1] * a + cw_ref[2:3, c0:c1] * a_next
              + cb_ref[:, c0:c1])
        hid = (_gelu_tanh(ac) * v).astype(BF16)
        part = _dot(hid, wd_ref[c0:c1, :])
        acc = part if acc is None else acc + part
    o_ref[0] = h + gate_ref[0] * (_rms(acc) * ng_ref[...])


def _hidden_chunks(hidden, size):
    return tuple((c0, min(c0 + size, hidden)) for c0 in range(0, hidden, size))


def _conv_ffn(h, g, sc, sh, wa, wv, cw, cb, wd, ng, gate, *, tm, seg):
    bt, length, d = h.shape
    hidden = wa.shape[1]
    per_batch = sc.shape[0] != 1
    mod_spec = pl.BlockSpec((1, 1, d), (lambda b, i: (b, 0, 0)) if per_batch else (lambda b, i: (0, 0, 0)))
    vec_d = pl.BlockSpec((1, d), lambda b, i: (0, 0))
    return pl.pallas_call(
        functools.partial(_ffn_kernel, seg=seg, hidden_chunks=_hidden_chunks(hidden, 1024)),
        grid=(bt, length // tm),
        in_specs=[pl.BlockSpec((1, tm, d), lambda b, i: (b, i, 0)),
                  vec_d, mod_spec, mod_spec,
                  _resident((d, hidden)), _resident((d, hidden)),
                  pl.BlockSpec((3, hidden), lambda b, i: (0, 0)),
                  pl.BlockSpec((1, hidden), lambda b, i: (0, 0)),
                  _resident((hidden, d)),
                  vec_d, mod_spec],
        out_specs=pl.BlockSpec((1, tm, d), lambda b, i: (b, i, 0)),
        out_shape=jax.ShapeDtypeStruct((bt, length, d), F32),
        compiler_params=_cparams("parallel", "parallel"),
        name="conv_glu_ffn",
    )(h, g.reshape(1, d), sc, sh, wa, wv, cw, cb.reshape(1, hidden), wd, ng.reshape(1, d), gate)


def _ret_kernel(lg_ref, qc_ref, kc_ref, vc_ref, gc_ref, ql_ref, kl_ref, vl_ref, gl_ref, gn_ref,
                rcos_ref, rsin_ref, ccos_ref, csin_ref, *rest, need_ctx, chunk):
    if need_ctx:
        yc_ref, yl_ref, qr_s, kr_s, qcs_s, kcs_s, ol_s, oc_s, sf_s, sb_s = rest
    else:
        yl_ref, qr_s, kr_s, qcs_s, kcs_s, ol_s, oc_s, sf_s, sb_s = rest
        yc_ref = None
    c = chunk
    head = pl.program_id(1)
    lgf = lg_ref[0, head]
    lgb = lg_ref[1, head]
    dk = ql_ref.shape[-1]
    k_scale = dk ** -0.5
    n_lat = ql_ref.shape[1] // c
    n_ctx = qc_ref.shape[1] // c

    ti = lax.broadcasted_iota(jnp.int32, (c, 1), 0).astype(F32)
    dq_f = jnp.exp(lgf * (ti + 1.0))
    dk_f = jnp.exp(lgf * (c - 1.0 - ti))
    dq_b = jnp.exp(lgb * (c - ti))
    dk_b = jnp.exp(lgb * ti)
    full_chunk = jnp.full((1, 1), float(c), F32)
    dc_f = jnp.exp(lgf * full_chunk)
    dc_b = jnp.exp(lgb * full_chunk)
    diff = (lax.broadcasted_iota(jnp.int32, (c, c), 0) - lax.broadcasted_iota(jnp.int32, (c, c), 1)).astype(F32)
    dmask = (jnp.where(diff >= 0, jnp.exp(lgf * jnp.maximum(diff, 0.0)), 0.0)
             + jnp.where(diff <= 0, jnp.exp(lgb * jnp.maximum(-diff, 0.0)), 0.0))

    half = dk // 2

    def rope_row(r, carry):
        rows = pl.ds(pl.multiple_of(r * GRID_W, GRID_W), GRID_W)
        cr = rcos_ref[pl.ds(r, 1), :]
        sr = rsin_ref[pl.ds(r, 1), :]
        for src, dst, scale in ((ql_ref, qr_s, 1.0), (kl_ref, kr_s, k_scale)):
            t = src[0, rows, :].astype(F32) * scale
            t0, t1 = t[:, :half], t[:, half:]
            dst[rows, :half] = (t0 * cr + pltpu.roll(t0, half // 2, 1) * sr).astype(BF16)
            dst[rows, half:] = (t1 * ccos_ref[...] + pltpu.roll(t1, half // 2, 1) * csin_ref[...]).astype(BF16)
        return carry

    lax.fori_loop(0, ql_ref.shape[1] // GRID_W, rope_row, 0)
    qcs_s[...] = qc_ref[0]
    kcs_s[...] = (kc_ref[0].astype(F32) * k_scale).astype(BF16)

    sf_s[...] = jnp.zeros_like(sf_s)
    sb_s[...] = jnp.zeros_like(sb_s)

    def scan(q_s, k_s, v_ref, o_s, n, with_out):
        if with_out:
            o_s[...] = jnp.zeros_like(o_s)

        def step(i, carry):
            fw = pl.ds(pl.multiple_of(i * c, c), c)
            bw = pl.ds(pl.multiple_of((n - 1 - i) * c, c), c)
            qf, kf, vf = q_s[fw, :], k_s[fw, :], v_ref[0, fw, :]
            if with_out:
                p = (_dot_nt(qf, kf) * dmask).astype(BF16)
                o_s[fw, :] += _dot(p, vf) + dq_f * _dot(qf, sf_s[...].astype(BF16))
            kd = (kf.astype(F32) * dk_f).astype(BF16)
            sf_s[...] = dc_f * sf_s[...] + _dot_tn(kd, vf)
            qb, kb, vb = q_s[bw, :], k_s[bw, :], v_ref[0, bw, :]
            if with_out:
                o_s[bw, :] += dq_b * _dot(qb, sb_s[...].astype(BF16))
            kdb = (kb.astype(F32) * dk_b).astype(BF16)
            sb_s[...] = dc_b * sb_s[...] + _dot_tn(kdb, vb)
            return carry

        lax.fori_loop(0, n, step, 0)

    def readout(o_s, g_ref, y_ref, n):
        def step(i, carry):
            rows = pl.ds(pl.multiple_of(i * c, c), c)
            o = o_s[rows, :]
            o = o - jnp.mean(o, axis=-1, keepdims=True)
            o = _rms(o) * gn_ref[...]
            y_ref[0, rows, :] = (_silu(g_ref[0, rows, :].astype(F32)) * o).astype(y_ref.dtype)
            return carry

        lax.fori_loop(0, n, step, 0)

    scan(qcs_s, kcs_s, vc_ref, oc_s, n_ctx, need_ctx)
    if need_ctx:
        readout(oc_s, gc_ref, yc_ref, n_ctx)
    scan(qr_s, kr_s, vl_ref, ol_s, n_lat, True)
    readout(ol_s, gl_ref, yl_ref, n_lat)


def _rope_tables(length, dk):
    nf = dk // 4
    inv = ROPE_BASE ** (-jnp.arange(nf, dtype=F32) / nf)

    def tab(p):
        ang = p[:, None] * inv
        cos, sin = jnp.cos(ang), jnp.sin(ang)
        return jnp.concatenate([cos, cos], -1), jnp.concatenate([-sin, sin], -1)

    rcos, rsin = tab(jnp.arange(length // GRID_W).astype(F32))
    ccos, csin = tab(jnp.arange(GRID_W).astype(F32))
    return rcos, rsin, ccos, csin


def _retention_scan(p_c, p_l, log_gamma, gn, need_ctx):
    bt, length, _ = p_l.shape
    n_c = p_c.shape[1]
    hv = gn.shape[0]
    dv = hv // RET_HEADS
    dk = dv // 2
    h = RET_HEADS
    tables = _rope_tables(length, dk)

    def cols(rows, width, off):
        return pl.BlockSpec((1, rows, width), lambda b, j: (b, 0, off + j))

    def specs(rows):
        return [cols(rows, dk, 0), cols(rows, dk, h), cols(rows, dv, h), cols(rows, dv, 2 * h)]

    in_specs = ([pl.BlockSpec(memory_space=pltpu.SMEM)] + specs(n_c) + specs(length)
                + [pl.BlockSpec((1, dv), lambda b, j: (0, j))]
                + [_resident(t.shape) for t in tables])
    out_specs = [cols(length, dv, 0)]
    out_shape = [jax.ShapeDtypeStruct((bt, length, hv), BF16)]
    if need_ctx:
        out_specs.insert(0, cols(n_c, dv, 0))
        out_shape.insert(0, jax.ShapeDtypeStruct((bt, n_c, hv), BF16))
    scratch = [pltpu.VMEM((length, dk), BF16), pltpu.VMEM((length, dk), BF16),
               pltpu.VMEM((n_c, dk), BF16), pltpu.VMEM((n_c, dk), BF16),
               pltpu.VMEM((length, dv), F32), pltpu.VMEM((n_c, dv), F32),
               pltpu.VMEM((dk, dv), F32), pltpu.VMEM((dk, dv), F32)]
    res = pl.pallas_call(
        functools.partial(_ret_kernel, need_ctx=need_ctx, chunk=RET_CHUNK),
        grid=(bt, h),
        in_specs=in_specs, out_specs=out_specs, out_shape=out_shape, scratch_shapes=scratch,
        compiler_params=_cparams("parallel", "parallel"),
        name="retention_scan",
    )(log_gamma, p_c, p_c, p_c, p_c, p_l, p_l, p_l, p_l, gn.reshape(1, hv), *tables)
    return (res[0], res[1]) if need_ctx else (None, res[0])


def _hg_kernel(qc_ref, ic_ref, gc_ref, ffc_ref, fbc_ref, ql_ref, il_ref, gl_ref, ffl_ref, fbl_ref,
               lb_ref, gn_ref, *rest, need_ctx, chunk):
    if need_ctx:
        yc_ref, yl_ref, ol_s, oc_s, sf_s, sb_s = rest
    else:
        yl_ref, ol_s, oc_s, sf_s, sb_s = rest
        yc_ref = None
    c = chunk
    n_lat = ql_ref.shape[1] // c
    n_ctx = qc_ref.shape[1] // c
    lb = lb_ref[...]
    one_m_lb = 1.0 - lb
    row = lax.broadcasted_iota(jnp.int32, (c, c), 0)
    col = lax.broadcasted_iota(jnp.int32, (c, c), 1)
    lower = row >= col
    upper = row <= col
    tril = lower.astype(F32)
    triu = upper.astype(F32)

    def gates(f):
        e = jnp.exp(-jnp.abs(f))
        r = 1.0 / (1.0 + e)
        pos = f >= 0
        sig = jnp.where(pos, r, e * r)
        sig_neg = jnp.where(pos, e * r, r)
        return jnp.log(lb + one_m_lb * sig), one_m_lb * sig_neg

    def half_step(q_ref, i_ref, f_ref, o_s, s_s, rows, tri, mask, last, with_out):
        la, kk = gates(f_ref[0, rows, :])
        b = jnp.dot(tri, la, precision=_HI, preferred_element_type=F32)
        b_end = b[last:last + 1, :]
        v = i_ref[0, rows, :]
        if with_out:
            qd = (_silu(q_ref[0, rows, :].astype(F32)) * jnp.exp(b)).astype(BF16)
            kd = (kk * jnp.exp(-b)).astype(BF16)
            s = jnp.where(mask, _dot_nt(qd, kd), 0.0).astype(BF16)
            o_s[rows, :] += _dot(s, v) + _dot_nt(qd, s_s[...].astype(BF16))
        kl = (kk * jnp.exp(b_end - b)).astype(BF16)
        s_s[...] = jnp.exp(b_end) * s_s[...] + _dot_tn(v, kl)

    def scan(q_ref, i_ref, ff_ref, fb_ref, o_s, n, with_out):
        if with_out:
            o_s[...] = jnp.zeros_like(o_s)

        def step(i, carry):
            fw = pl.ds(pl.multiple_of(i * c, c), c)
            bw = pl.ds(pl.multiple_of((n - 1 - i) * c, c), c)
            half_step(q_ref, i_ref, ff_ref, o_s, sf_s, fw, tril, lower, c - 1, with_out)
            half_step(q_ref, i_ref, fb_ref, o_s, sb_s, bw, triu, upper, 0, with_out)
            return carry

        lax.fori_loop(0, n, step, 0)

    def readout(o_s, g_ref, y_ref, n):
        def step(i, carry):
            rows = pl.ds(pl.multiple_of(i * c, c), c)
            o = _rms(o_s[rows, :]) * gn_ref[...]
            y_ref[0, rows, :] = (_silu(g_ref[0, rows, :].astype(F32)) * o).astype(y_ref.dtype)
            return carry

        lax.fori_loop(0, n, step, 0)

    sf_s[...] = jnp.zeros_like(sf_s)
    sb_s[...] = jnp.zeros_like(sb_s)
    scan(qc_ref, ic_ref, ffc_ref, fbc_ref, oc_s, n_ctx, need_ctx)
    if need_ctx:
        readout(oc_s, gc_ref, yc_ref, n_ctx)
    scan(ql_ref, il_ref, ffl_ref, fbl_ref, ol_s, n_lat, True)
    readout(ol_s, gl_ref, yl_ref, n_lat)


def _hgrn2_scan(p_c, f_c, p_l, f_l, lb, gn, need_ctx):
    bt, length, d3 = p_l.shape
    d = d3 // 3
    n_c = p_c.shape[1]
    h = d // HG_DK

    def cols(rows, off):
        return pl.BlockSpec((1, rows, HG_DK), lambda b, j: (b, 0, off + j))

    def specs(rows):
        return [cols(rows, 0), cols(rows, h), cols(rows, 2 * h), cols(rows, 0), cols(rows, h)]

    vec = pl.BlockSpec((1, HG_DK), lambda b, j: (0, j))
    out_specs = [cols(length, 0)]
    out_shape = [jax.ShapeDtypeStruct((bt, length, d), BF16)]
    if need_ctx:
        out_specs.insert(0, cols(n_c, 0))
        out_shape.insert(0, jax.ShapeDtypeStruct((bt, n_c, d), BF16))
    scratch = [pltpu.VMEM((length, HG_DK), F32), pltpu.VMEM((n_c, HG_DK), F32),
               pltpu.VMEM((HG_DK, HG_DK), F32), pltpu.VMEM((HG_DK, HG_DK), F32)]
    res = pl.pallas_call(
        functools.partial(_hg_kernel, need_ctx=need_ctx, chunk=HG_CHUNK),
        grid=(bt, h),
        in_specs=specs(n_c) + specs(length) + [vec, vec],
        out_specs=out_specs, out_shape=out_shape, scratch_shapes=scratch,
        compiler_params=_cparams("parallel", "parallel"),
        name="hgrn2_scan",
    )(p_c, p_c, p_c, f_c, f_c, p_l, p_l, p_l, f_l, f_l, lb.reshape(1, d), gn.reshape(1, d))
    return (res[0], res[1]) if need_ctx else (None, res[0])


def _split_dot(x, e):
    hi = x.astype(BF16)
    lo = (x - hi.astype(F32)).astype(BF16)
    return _dot(hi, e) + _dot(lo, e)


def _m2_kernel(zc_ref, xc_ref, bc_ref, cc_ref, dtc_ref, zl_ref, xl_ref, bl_ref, cl_ref, dtl_ref,
               cwx_ref, cwb_ref, cwc_ref, cbx_ref, cbb_ref, cbc_ref,
               sel_ref, bias_ref, aneg_ref, exp_ref, dskip_ref, gn_ref, *rest, need_ctx, chunk):
    if need_ctx:
        yc_ref, yl_ref, xl_s, bl_s, cl_s, xc_s, bc_s, cc_s, ol_s, oc_s, st_s = rest
    else:
        yl_ref, xl_s, bl_s, cl_s, xc_s, bc_s, cc_s, ol_s, oc_s, st_s = rest
        yc_ref = None
    c = chunk
    n_lat = xl_ref.shape[1] // c
    n_ctx = xc_ref.shape[1] // c
    hpg = xl_ref.shape[-1] // M2_HEADDIM
    gw = hpg * M2_HEADDIM
    row = lax.broadcasted_iota(jnp.int32, (c, c), 0)
    col = lax.broadcasted_iota(jnp.int32, (c, c), 1)
    lower = row >= col
    upper = row <= col
    tril = lower.astype(F32)
    triu = upper.astype(F32)
    ridx = lax.broadcasted_iota(jnp.int32, (c, 1), 0)

    def conv_silu(src_ref, dst_s, w_ref, b_ref, n):
        total = n * c
        halo = 16

        def step(i, carry):
            start = pl.multiple_of(i * c, c)
            rows = pl.ds(start, c)
            x = src_ref[0, rows, :].astype(F32)
            prev_blk = src_ref[0, pl.ds(pl.multiple_of(jnp.maximum(start - halo, 0), halo), halo), :]
            next_blk = src_ref[0, pl.ds(pl.multiple_of(jnp.minimum(start + c, total - halo), halo), halo), :]
            prev_row = jnp.where(i > 0, prev_blk[halo - 1:halo, :].astype(F32), 0.0)
            next_row = jnp.where(i < n - 1, next_blk[0:1, :].astype(F32), 0.0)
            x_prev = jnp.where(ridx == 0, prev_row, pltpu.roll(x, 1, 0))
            x_next = jnp.where(ridx == c - 1, next_row, pltpu.roll(x, c - 1, 0))
            y = w_ref[0:1, :] * x_prev + w_ref[1:2, :] * x + w_ref[2:3, :] * x_next + b_ref[...]
            dst_s[rows, :] = _silu(y).astype(dst_s.dtype)
            return carry

        lax.fori_loop(0, n, step, 0)

    def gates(dt_ref, rows):
        raw = jnp.dot(dt_ref[0, rows, :], sel_ref[0], precision=_HI, preferred_element_type=F32)
        x = raw + bias_ref[0]
        dt = jnp.maximum(x, 0.0) + jnp.log(1.0 + jnp.exp(-jnp.abs(x)))
        la = dt * aneg_ref[0]
        return dt, la

    def fwd_half(x_s, b_s, c_s, dt_ref, o_s, rows):
        dt, la = gates(dt_ref, rows)
        bf = jnp.dot(tril, la, precision=_HI, preferred_element_type=F32)
        bb = jnp.dot(triu, la, precision=_HI, preferred_element_type=F32)
        bf_t, bb_t, dt_t = bf.T, bb.T, dt.T
        cm, bm, xs = c_s[rows, :], b_s[rows, :], x_s[rows, :]
        base = _dot_nt(cm, bm)
        ef = _split_dot(jnp.exp(bf), exp_ref[:, :gw])
        wf = _split_dot(jnp.exp(bf[c - 1:c, :] - bf) * dt, exp_ref[:, :gw])
        sf = st_s[:, :gw]
        o_s[rows, :] += ef * _dot(cm, sf.astype(BF16))
        for r in range(hpg):
            rb = hpg + r
            hd = slice(r * M2_HEADDIM, (r + 1) * M2_HEADDIM)
            mf = jnp.where(lower, jnp.exp(jnp.minimum(bf[:, r:r + 1] - bf_t[r:r + 1, :], 0.0)) * dt_t[r:r + 1, :], 0.0)
            mb = jnp.where(upper, jnp.exp(jnp.minimum(bb[:, rb:rb + 1] - bb_t[rb:rb + 1, :], 0.0)) * dt_t[rb:rb + 1, :],
                           0.0)
            m = (base * (mf + mb)).astype(BF16)
            o_s[rows, hd] += _dot(m, x_s[rows, hd])
        xw = (xs.astype(F32) * wf).astype(BF16)
        st_s[:, :gw] = ef[c - 1:c, :] * sf + _dot_tn(bm, xw)

    def bwd_half(x_s, b_s, c_s, dt_ref, o_s, rows, with_out):
        dt, la = gates(dt_ref, rows)
        bb = jnp.dot(triu, la, precision=_HI, preferred_element_type=F32)
        cm, bm, xs = c_s[rows, :], b_s[rows, :], x_s[rows, :]
        eb = _split_dot(jnp.exp(bb), exp_ref[:, gw:])
        wb = _split_dot(jnp.exp(bb[0:1, :] - bb) * dt, exp_ref[:, gw:])
        sb = st_s[:, gw:]
        if with_out:
            o_s[rows, :] += eb * _dot(cm, sb.astype(BF16))
        xw = (xs.astype(F32) * wb).astype(BF16)
        st_s[:, gw:] = eb[0:1, :] * sb + _dot_tn(bm, xw)

    def fwd_state_only(x_s, b_s, dt_ref, rows):
        dt, la = gates(dt_ref, rows)
        bf = jnp.dot(tril, la, precision=_HI, preferred_element_type=F32)
        bm, xs = b_s[rows, :], x_s[rows, :]
        ef_end = _split_dot(jnp.exp(bf), exp_ref[:, :gw])[c - 1:c, :]
        wf = _split_dot(jnp.exp(bf[c - 1:c, :] - bf) * dt, exp_ref[:, :gw])
        xw = (xs.astype(F32) * wf).astype(BF16)
        st_s[:, :gw] = ef_end * st_s[:, :gw] + _dot_tn(bm, xw)

    def scan(x_s, b_s, c_s, dt_ref, o_s, n, with_out):
        if with_out:
            o_s[...] = jnp.zeros_like(o_s)

        def step(i, carry):
            fw = pl.ds(pl.multiple_of(i * c, c), c)
            bw = pl.ds(pl.multiple_of((n - 1 - i) * c, c), c)
            if with_out:
                fwd_half(x_s, b_s, c_s, dt_ref, o_s, fw)
            else:
                fwd_state_only(x_s, b_s, dt_ref, fw)
            bwd_half(x_s, b_s, c_s, dt_ref, o_s, bw, with_out)
            return carry

        lax.fori_loop(0, n, step, 0)

    def readout(o_s, x_s, z_ref, y_ref, n):
        def step(i, carry):
            rows = pl.ds(pl.multiple_of(i * c, c), c)
            y = o_s[rows, :] + dskip_ref[...] * x_s[rows, :].astype(F32)
            y = y * _silu(z_ref[0, rows, :].astype(F32))
            y_ref[0, rows, :] = (_rms(y) * gn_ref[...]).astype(y_ref.dtype)
            return carry

        lax.fori_loop(0, n, step, 0)

    conv_silu(xc_ref, xc_s, cwx_ref, cbx_ref, n_ctx)
    conv_silu(bc_ref, bc_s, cwb_ref, cbb_ref, n_ctx)
    conv_silu(cc_ref, cc_s, cwc_ref, cbc_ref, n_ctx)
    conv_silu(xl_ref, xl_s, cwx_ref, cbx_ref, n_lat)
    conv_silu(bl_ref, bl_s, cwb_ref, cbb_ref, n_lat)
    conv_silu(cl_ref, cl_s, cwc_ref, cbc_ref, n_lat)
    st_s[...] = jnp.zeros_like(st_s)
    scan(xc_s, bc_s, cc_s, dtc_ref, oc_s, n_ctx, need_ctx)
    if need_ctx:
        readout(oc_s, xc_s, zc_ref, yc_ref, n_ctx)
    scan(xl_s, bl_s, cl_s, dtl_ref, ol_s, n_lat, True)
    readout(ol_s, xl_s, zl_ref, yl_ref, n_lat)


def _mamba2_scan(p_c, dt_c, p_l, dt_l, conv_w, conv_b, dt_bias, a_neg, d_skip, gn, need_ctx):
    bt, length, _ = p_l.shape
    n_c = p_c.shape[1]
    dinner = gn.shape[0]
    heads = dinner // M2_HEADDIM
    g = M2_GROUPS
    hpg = heads // g
    gw = dinner // g
    ns = M2_DSTATE
    conv_dim = conv_w.shape[1]
    zb = dinner // gw
    xb = 2 * dinner // ns

    lane = jnp.arange(LANES)
    src = jnp.arange(LANES)
    grp = jnp.arange(g)
    want = jnp.where(lane < hpg, grp[:, None] * hpg + lane,
                     jnp.where(lane < 2 * hpg, heads + grp[:, None] * hpg + lane - hpg, -1))
    sel = (src[None, :, None] == want[:, None, :]).astype(F32)
    flat_bias = jnp.concatenate([dt_bias.reshape(-1).astype(F32), jnp.zeros((LANES - 2 * heads,), F32)])
    flat_aneg = jnp.concatenate([a_neg.reshape(-1).astype(F32), jnp.zeros((LANES - 2 * heads,), F32)])
    bias_sel = jnp.einsum('j,gjm->gm', flat_bias, sel, precision=_HI)[:, None, :]
    aneg_sel = jnp.einsum('j,gjm->gm', flat_aneg, sel, precision=_HI)[:, None, :]
    expand = (jnp.arange(2 * gw)[None, :] // M2_HEADDIM == jnp.arange(LANES)[:, None]).astype(BF16)
    d_exp = jnp.repeat(d_skip.astype(F32), M2_HEADDIM).reshape(1, dinner)

    def cols(rows, width, off):
        return pl.BlockSpec((1, rows, width), lambda b, j: (b, 0, off + j))

    def specs(rows):
        return [cols(rows, gw, 0), cols(rows, gw, zb), cols(rows, ns, xb), cols(rows, ns, xb + g),
                pl.BlockSpec((1, rows, LANES), lambda b, j: (b, 0, 0))]

    def cvec(nrows, width, off):
        return pl.BlockSpec((nrows, width), lambda b, j: (0, off + j))

    per_group = pl.BlockSpec((1, 1, LANES), lambda b, j: (j, 0, 0))
    in_specs = (specs(n_c) + specs(length)
                + [cvec(3, gw, 0), cvec(3, ns, dinner // ns), cvec(3, ns, dinner // ns + g),
                   cvec(1, gw, 0), cvec(1, ns, dinner // ns), cvec(1, ns, dinner // ns + g),
                   pl.BlockSpec((1, LANES, LANES), lambda b, j: (j, 0, 0)), per_group, per_group,
                   _resident(expand.shape), cvec(1, gw, 0), cvec(1, gw, 0)])
    out_specs = [cols(length, gw, 0)]
    out_shape = [jax.ShapeDtypeStruct((bt, length, dinner), BF16)]
    if need_ctx:
        out_specs.insert(0, cols(n_c, gw, 0))
        out_shape.insert(0, jax.ShapeDtypeStruct((bt, n_c, dinner), BF16))
    scratch = [pltpu.VMEM((length, gw), BF16), pltpu.VMEM((length, ns), BF16), pltpu.VMEM((length, ns), BF16),
               pltpu.VMEM((n_c, gw), BF16), pltpu.VMEM((n_c, ns), BF16), pltpu.VMEM((n_c, ns), BF16),
               pltpu.VMEM((length, gw), F32), pltpu.VMEM((n_c, gw), F32),
               pltpu.VMEM((ns, 2 * gw), F32)]
    cb = conv_b.reshape(1, conv_dim)
    res = pl.pallas_call(
        functools.partial(_m2_kernel, need_ctx=need_ctx, chunk=M2_CHUNK),
        grid=(bt, g),
        in_specs=in_specs, out_specs=out_specs, out_shape=out_shape, scratch_shapes=scratch,
        compiler_params=_cparams("parallel", "parallel"),
        name="mamba2_scan",
    )(p_c, p_c, p_c, p_c, dt_c, p_l, p_l, p_l, p_l, dt_l,
      conv_w, conv_w, conv_w, cb, cb, cb, sel, bias_sel, aneg_sel, expand, d_exp, gn.reshape(1, dinner))
    return (res[0], res[1]) if need_ctx else (None, res[0])


def kernel(x, c, ctx, c_ctx, ada_w, ada_b, norm_g, ret_w_in, ret_w_out, ret_decay, ret_gn, hg_w_in, hg_w_out, hg_lb, hg_gn, m2_w_in, m2_w_out, m2_conv_w, m2_conv_b, m2_dt_bias, m2_a_log, m2_d, m2_gn, ffn_w_up, ffn_conv_w, ffn_conv_b, ffn_w_down):
    bt, length, d = x.shape
    n_c = ctx.shape[1]
    depth = ada_w.shape[0]
    hidden = ffn_conv_w.shape[-1]
    tm = min(512, length)
    tm_c = n_c

    rows = -(-(bt + 1) // 8) * 8
    cc = jnp.concatenate([c, c_ctx[None, :], jnp.zeros((rows - bt - 1, d), F32)], axis=0)
    mod = _modulation(cc, ada_w, ada_b)

    lb_cum = jnp.cumsum(jax.nn.softmax(hg_lb.astype(F32), axis=0), axis=0)
    lb_all = lb_cum - lb_cum[0]

    h_l, h_c = x, ctx
    for i in range(depth):
        kind, j = i % N_MIXERS, i // N_MIXERS
        need_ctx = i < depth - 1
        mod_l = mod[i, :bt].reshape(bt, 6, 1, d)
        mod_c = mod[i, bt].reshape(1, 6, 1, d)
        sh1_l, sc1_l, g1_l, sh2_l, sc2_l, g2_l = (mod_l[:, k] for k in range(6))
        sh1_c, sc1_c, g1_c, sh2_c, sc2_c, g2_c = (mod_c[:, k] for k in range(6))

        if kind == 0:
            w_in = ret_w_in[j].astype(BF16)
            p_l = _project(h_l, norm_g[i, 0], sc1_l, sh1_l, w_in, tm=tm)
            p_c = _project(h_c, norm_g[i, 0], sc1_c, sh1_c, w_in, tm=tm_c)
            log_gamma = jax.nn.log_sigmoid(ret_decay[j].astype(F32))
            y_c, y_l = _retention_scan(p_c, p_l, log_gamma, ret_gn[j], need_ctx)
            w_out = ret_w_out[j].astype(BF16)
        elif kind == 1:
            wq, wff, wfb, wi, wg = jnp.split(hg_w_in[j], 5, axis=-1)
            w_main = jnp.concatenate([wq, wi, wg], axis=-1).astype(BF16)
            w_gate = jnp.concatenate([wff, wfb], axis=-1).astype(BF16)
            p_l, f_l = _project(h_l, norm_g[i, 0], sc1_l, sh1_l, w_main, w_gate, tm=tm)
            p_c, f_c = _project(h_c, norm_g[i, 0], sc1_c, sh1_c, w_main, w_gate, tm=tm_c)
            y_c, y_l = _hgrn2_scan(p_c, f_c, p_l, f_l, lb_all[i], hg_gn[j], need_ctx)
            w_out = hg_w_out[j].astype(BF16)
        else:
            dinner = m2_gn.shape[-1]
            n_main = dinner + m2_conv_w.shape[-1]
            w_main = m2_w_in[j][:, :n_main].astype(BF16)
            w_dt = m2_w_in[j][:, n_main:]
            w_dt = jnp.concatenate([w_dt, jnp.zeros((d, LANES - w_dt.shape[1]), w_dt.dtype)], axis=-1).astype(BF16)
            p_l, dt_l = _project(h_l, norm_g[i, 0], sc1_l, sh1_l, w_main, w_dt, tm=tm)
            p_c, dt_c = _project(h_c, norm_g[i, 0], sc1_c, sh1_c, w_main, w_dt, tm=tm_c)
            a_neg = -jnp.exp(m2_a_log[j].astype(F32))
            y_c, y_l = _mamba2_scan(p_c, dt_c, p_l, dt_l, m2_conv_w[j], m2_conv_b[j], m2_dt_bias[j], a_neg,
                                    m2_d[j], m2_gn[j], need_ctx)
            w_out = m2_w_out[j].astype(BF16)

        wa = ffn_w_up[i][:, :hidden].astype(BF16)
        wv = ffn_w_up[i][:, hidden:].astype(BF16)
        wd = ffn_w_down[i].astype(BF16)
        h_l = _out_project(y_l, w_out, h_l, norm_g[i, 1], g1_l, tm=tm)
        h_l = _conv_ffn(h_l, norm_g[i, 2], sc2_l, sh2_l, wa, wv, ffn_conv_w[i], ffn_conv_b[i], wd,
                        norm_g[i, 3], g2_l, tm=tm, seg=GRID_W)
        if need_ctx:
            h_c = _out_project(y_c, w_out, h_c, norm_g[i, 1], g1_c, tm=tm_c)
            h_c = _conv_ffn(h_c, norm_g[i, 2], sc2_c, sh2_c, wa, wv, ffn_conv_w[i], ffn_conv_b[i], wd,
                            norm_g[i, 3], g2_c, tm=tm_c, seg=n_c)
    return h_l
```

```python
import functools
import math

import jax
import jax.numpy as jnp
from jax import lax
from jax.experimental import pallas as pl
from jax.experimental.pallas import tpu as pltpu

F32 = jnp.float32
BF16 = jnp.bfloat16

NORM_EPS = 1e-6
GRID_W = 64
ROPE_BASE = 10000.0
N_MIXERS = 3

RET_HEADS = 4
RET_CHUNK = 256
HG_DK = 128
HG_CHUNK = 64
HG_BLOCK = 512
HG_SAFE_LOG_DECAY = -80.0
M2_HEADDIM = 64
M2_GROUPS = 4
M2_DSTATE = 128
M2_CHUNK = 128
LANES = 128

V7X_VMEM_LIMIT_BYTES = 56 * 1024 * 1024

_HI = lax.Precision.HIGHEST


def _cparams(*sem):
    return pltpu.CompilerParams(dimension_semantics=sem, vmem_limit_bytes=V7X_VMEM_LIMIT_BYTES)


def _sigmoid(x):
    return 1.0 / (1.0 + jnp.exp(-x))


def _silu(x):
    return x * _sigmoid(x)


def _rms(t):
    return t * lax.rsqrt(jnp.mean(t * t, axis=-1, keepdims=True) + NORM_EPS)


def _dot(a, b):
    return jnp.dot(a, b, preferred_element_type=F32)


def _dot_nt(a, b):
    return lax.dot_general(a, b, (((1,), (1,)), ((), ())), preferred_element_type=F32)


def _dot_tn(a, b):
    return lax.dot_general(a, b, (((0,), (0,)), ((), ())), preferred_element_type=F32)


def _resident(shape):
    return pl.BlockSpec(shape, lambda *_: (0,) * len(shape), pipeline_mode=pl.Buffered(1))


def _mod_kernel(c_ref, w_ref, b_ref, o_ref):
    s = _silu(c_ref[...]).astype(BF16)
    o_ref[0] = _dot(s, w_ref[0].astype(BF16)) + b_ref[0]


def _modulation(cc, ada_w, ada_b):
    depth, d, n = ada_w.shape
    rows = cc.shape[0]
    tn = 1536
    return pl.pallas_call(
        _mod_kernel,
        grid=(depth, n // tn),
        in_specs=[pl.BlockSpec((rows, d), lambda i, j: (0, 0)),
                  pl.BlockSpec((1, d, tn), lambda i, j: (i, 0, j)),
                  pl.BlockSpec((1, 1, tn), lambda i, j: (i, 0, j))],
        out_specs=pl.BlockSpec((1, rows, tn), lambda i, j: (i, 0, j)),
        out_shape=jax.ShapeDtypeStruct((depth, rows, n), F32),
        compiler_params=_cparams("parallel", "parallel"),
        name="adaln_mod",
    )(cc, ada_w, ada_b.reshape(depth, 1, n))


def _norm_mod(x, g, sc, sh):
    return _rms(x) * (g * (1.0 + sc)) + sh


def _proj_kernel(x_ref, g_ref, sc_ref, sh_ref, w_ref, *rest, ncol):
    u = _norm_mod(x_ref[0], g_ref[...], sc_ref[0], sh_ref[0]).astype(BF16)
    if len(rest) == 3:
        w2_ref, o_ref, o2_ref = rest
        o2_ref[0] = _dot(u, w2_ref[...])
    else:
        (o_ref,) = rest
    n = w_ref.shape[1]
    for n0 in range(0, n, ncol):
        n1 = min(n0 + ncol, n)
        o_ref[0, :, n0:n1] = _dot(u, w_ref[:, n0:n1]).astype(o_ref.dtype)


def _project(h, g, sc, sh, w, w2=None, *, tm):
    bt, length, d = h.shape
    n = w.shape[1]
    per_batch = sc.shape[0] != 1
    mod_spec = pl.BlockSpec((1, 1, d), (lambda b, i: (b, 0, 0)) if per_batch else (lambda b, i: (0, 0, 0)))
    in_specs = [pl.BlockSpec((1, tm, d), lambda b, i: (b, i, 0)),
                pl.BlockSpec((1, d), lambda b, i: (0, 0)),
                mod_spec, mod_spec, _resident((d, n))]
    out_specs = [pl.BlockSpec((1, tm, n), lambda b, i: (b, i, 0))]
    out_shape = [jax.ShapeDtypeStruct((bt, length, n), BF16)]
    args = [h, g.reshape(1, d), sc, sh, w]
    if w2 is not None:
        n2 = w2.shape[1]
        in_specs.append(_resident((d, n2)))
        out_specs.append(pl.BlockSpec((1, tm, n2), lambda b, i: (b, i, 0)))
        out_shape.append(jax.ShapeDtypeStruct((bt, length, n2), F32))
        args.append(w2)
    res = pl.pallas_call(
        functools.partial(_proj_kernel, ncol=512),
        grid=(bt, length // tm),
        in_specs=in_specs, out_specs=out_specs, out_shape=out_shape,
        compiler_params=_cparams("parallel", "parallel"),
        name="norm_mod_proj",
    )(*args)
    return res if w2 is not None else res[0]


def _out_kernel(y_ref, w_ref, h_ref, ng_ref, gate_ref, o_ref):
    t = _dot(y_ref[0], w_ref[...])
    o_ref[0] = h_ref[0] + gate_ref[0] * (_rms(t) * ng_ref[...])


def _out_project(y, w, h, ng, gate, *, tm):
    bt, length, d = h.shape
    dy = y.shape[-1]
    per_batch = gate.shape[0] != 1
    gate_spec = pl.BlockSpec((1, 1, d), (lambda b, i: (b, 0, 0)) if per_batch else (lambda b, i: (0, 0, 0)))
    return pl.pallas_call(
        _out_kernel,
        grid=(bt, length // tm),
        in_specs=[pl.BlockSpec((1, tm, dy), lambda b, i: (b, i, 0)),
                  _resident((dy, d)),
                  pl.BlockSpec((1, tm, d), lambda b, i: (b, i, 0)),
                  pl.BlockSpec((1, d), lambda b, i: (0, 0)),
                  gate_spec],
        out_specs=pl.BlockSpec((1, tm, d), lambda b, i: (b, i, 0)),
        out_shape=jax.ShapeDtypeStruct((bt, length, d), F32),
        compiler_params=_cparams("parallel", "parallel"),
        name="out_proj_residual",
    )(y, w, h, ng.reshape(1, d), gate)


def _gelu_tanh(x):
    return 0.5 * x * (1.0 + jnp.tanh(math.sqrt(2.0 / math.pi) * (x + 0.044715 * (x * x * x))))


def _ffn_kernel(h_ref, g_ref, sc_ref, sh_ref, wa_ref, wv_ref, cw_ref, cb_ref, wd_ref, ng_ref, gate_ref, o_ref,
                *, seg, hidden_chunks):
    h = h_ref[0]
    tm = h.shape[0]
    u = _norm_mod(h, g_ref[...], sc_ref[0], sh_ref[0]).astype(BF16)
    pos = lax.broadcasted_iota(jnp.int32, (tm, 1), 0) % seg
    has_prev = pos != 0
    has_next = pos != seg - 1
    acc = None
    for c0, c1 in hidden_chunks:
        a = _dot(u, wa_ref[:, c0:c1])
        v = _dot(u, wv_ref[:, c0:c1])
        a_prev = jnp.where(has_prev, pltpu.roll(a, 1, 0), 0.0)
        a_next = jnp.where(has_next, pltpu.roll(a, tm - 1, 0), 0.0)
        ac = (cw_ref[0:1, c0:c1] * a_prev + cw_ref[1:2, c0:c1] * a + cw_ref[2:3, c0:c1] * a_next
              + cb_ref[:, c0:c1])
        hid = (_gelu_tanh(ac) * v).astype(BF16)
        part = _dot(hid, wd_ref[c0:c1, :])
        acc = part if acc is None else acc + part
    o_ref[0] = h + gate_ref[0] * (_rms(acc) * ng_ref[...])


def _hidden_chunks(hidden, size):
    return tuple((c0, min(c0 + size, hidden)) for c0 in range(0, hidden, size))


def _conv_ffn(h, g, sc, sh, wa, wv, cw, cb, wd, ng, gate, *, tm, seg):
    bt, length, d = h.shape
    hidden = wa.shape[1]
    per_batch = sc.shape[0] != 1
    mod_spec = pl.BlockSpec((1, 1, d), (lambda b, i: (b, 0, 0)) if per_batch else (lambda b, i: (0, 0, 0)))
    vec_d = pl.BlockSpec((1, d), lambda b, i: (0, 0))
    return pl.pallas_call(
        functools.partial(_ffn_kernel, seg=seg, hidden_chunks=_hidden_chunks(hidden, 1024)),
        grid=(bt, length // tm),
        in_specs=[pl.BlockSpec((1, tm, d), lambda b, i: (b, i, 0)),
                  vec_d, mod_spec, mod_spec,
                  _resident((d, hidden)), _resident((d, hidden)),
                  pl.BlockSpec((3, hidden), lambda b, i: (0, 0)),
                  pl.BlockSpec((1, hidden), lambda b, i: (0, 0)),
                  _resident((hidden, d)),
                  vec_d, mod_spec],
        out_specs=pl.BlockSpec((1, tm, d), lambda b, i: (b, i, 0)),
        out_shape=jax.ShapeDtypeStruct((bt, length, d), F32),
        compiler_params=_cparams("parallel", "parallel"),
        name="conv_glu_ffn",
    )(h, g.reshape(1, d), sc, sh, wa, wv, cw, cb.reshape(1, hidden), wd, ng.reshape(1, d), gate)


def _ret_kernel(lg_ref, qc_ref, kc_ref, vc_ref, gc_ref, ql_ref, kl_ref, vl_ref, gl_ref, gn_ref,
                rcos_ref, rsin_ref, ccos_ref, csin_ref, *rest, need_ctx, chunk):
    if need_ctx:
        yc_ref, yl_ref, qr_s, kr_s, qcs_s, kcs_s, ol_s, oc_s, sf_s, sb_s = rest
    else:
        yl_ref, qr_s, kr_s, qcs_s, kcs_s, ol_s, oc_s, sf_s, sb_s = rest
        yc_ref = None
    c = chunk
    head = pl.program_id(1)
    lgf = lg_ref[0, head]
    lgb = lg_ref[1, head]
    dk = ql_ref.shape[-1]
    k_scale = dk ** -0.5
    n_lat = ql_ref.shape[1] // c
    n_ctx = qc_ref.shape[1] // c

    ti = lax.broadcasted_iota(jnp.int32, (c, 1), 0).astype(F32)
    dq_f = jnp.exp(lgf * (ti + 1.0))
    dk_f = jnp.exp(lgf * (c - 1.0 - ti))
    dq_b = jnp.exp(lgb * (c - ti))
    dk_b = jnp.exp(lgb * ti)
    full_chunk = jnp.full((1, 1), float(c), F32)
    dc_f = jnp.exp(lgf * full_chunk)
    dc_b = jnp.exp(lgb * full_chunk)
    diff = (lax.broadcasted_iota(jnp.int32, (c, c), 0) - lax.broadcasted_iota(jnp.int32, (c, c), 1)).astype(F32)
    dmask = (jnp.where(diff >= 0, jnp.exp(lgf * jnp.maximum(diff, 0.0)), 0.0)
             + jnp.where(diff <= 0, jnp.exp(lgb * jnp.maximum(-diff, 0.0)), 0.0))

    half = dk // 2

    def rope_row(r, carry):
        rows = pl.ds(pl.multiple_of(r * GRID_W, GRID_W), GRID_W)
        cr = rcos_ref[pl.ds(r, 1), :]
        sr = rsin_ref[pl.ds(r, 1), :]
        for src, dst, scale in ((ql_ref, qr_s, 1.0), (kl_ref, kr_s, k_scale)):
            t = src[0, rows, :].astype(F32) * scale
            t0, t1 = t[:, :half], t[:, half:]
            dst[rows, :half] = (t0 * cr + pltpu.roll(t0, half // 2, 1) * sr).astype(BF16)
            dst[rows, half:] = (t1 * ccos_ref[...] + pltpu.roll(t1, half // 2, 1) * csin_ref[...]).astype(BF16)
        return carry

    lax.fori_loop(0, ql_ref.shape[1] // GRID_W, rope_row, 0)
    qcs_s[...] = qc_ref[0]
    kcs_s[...] = (kc_ref[0].astype(F32) * k_scale).astype(BF16)

    sf_s[...] = jnp.zeros_like(sf_s)
    sb_s[...] = jnp.zeros_like(sb_s)

    def scan(q_s, k_s, v_ref, o_s, n, with_out):
        if with_out:
            o_s[...] = jnp.zeros_like(o_s)

        def step(i, carry):
            fw = pl.ds(pl.multiple_of(i * c, c), c)
            bw = pl.ds(pl.multiple_of((n - 1 - i) * c, c), c)
            qf, kf, vf = q_s[fw, :], k_s[fw, :], v_ref[0, fw, :]
            if with_out:
                p = (_dot_nt(qf, kf) * dmask).astype(BF16)
                o_s[fw, :] += _dot(p, vf) + dq_f * _dot(qf, sf_s[...].astype(BF16))
            kd = (kf.astype(F32) * dk_f).astype(BF16)
            sf_s[...] = dc_f * sf_s[...] + _dot_tn(kd, vf)
            qb, kb, vb = q_s[bw, :], k_s[bw, :], v_ref[0, bw, :]
            if with_out:
                o_s[bw, :] += dq_b * _dot(qb, sb_s[...].astype(BF16))
            kdb = (kb.astype(F32) * dk_b).astype(BF16)
            sb_s[...] = dc_b * sb_s[...] + _dot_tn(kdb, vb)
            return carry

        lax.fori_loop(0, n, step, 0)

    def readout(o_s, g_ref, y_ref, n):
        def step(i, carry):
            rows = pl.ds(pl.multiple_of(i * c, c), c)
            o = o_s[rows, :]
            o = o - jnp.mean(o, axis=-1, keepdims=True)
            o = _rms(o) * gn_ref[...]
            y_ref[0, rows, :] = (_silu(g_ref[0, rows, :].astype(F32)) * o).astype(y_ref.dtype)
            return carry

        lax.fori_loop(0, n, step, 0)

    scan(qcs_s, kcs_s, vc_ref, oc_s, n_ctx, need_ctx)
    if need_ctx:
        readout(oc_s, gc_ref, yc_ref, n_ctx)
    scan(qr_s, kr_s, vl_ref, ol_s, n_lat, True)
    readout(ol_s, gl_ref, yl_ref, n_lat)


def _rope_tables(length, dk):
    nf = dk // 4
    inv = ROPE_BASE ** (-jnp.arange(nf, dtype=F32) / nf)

    def tab(p):
        ang = p[:, None] * inv
        cos, sin = jnp.cos(ang), jnp.sin(ang)
        return jnp.concatenate([cos, cos], -1), jnp.concatenate([-sin, sin], -1)

    rcos, rsin = tab(jnp.arange(length // GRID_W).astype(F32))
    ccos, csin = tab(jnp.arange(GRID_W).astype(F32))
    return rcos, rsin, ccos, csin


def _retention_scan(p_c, p_l, log_gamma, gn, need_ctx):
    bt, length, _ = p_l.shape
    n_c = p_c.shape[1]
    hv = gn.shape[0]
    dv = hv // RET_HEADS
    dk = dv // 2
    h = RET_HEADS
    tables = _rope_tables(length, dk)

    def cols(rows, width, off):
        return pl.BlockSpec((1, rows, width), lambda b, j: (b, 0, off + j))

    def specs(rows):
        return [cols(rows, dk, 0), cols(rows, dk, h), cols(rows, dv, h), cols(rows, dv, 2 * h)]

    in_specs = ([pl.BlockSpec(memory_space=pltpu.SMEM)] + specs(n_c) + specs(length)
                + [pl.BlockSpec((1, dv), lambda b, j: (0, j))]
                + [_resident(t.shape) for t in tables])
    out_specs = [cols(length, dv, 0)]
    out_shape = [jax.ShapeDtypeStruct((bt, length, hv), BF16)]
    if need_ctx:
        out_specs.insert(0, cols(n_c, dv, 0))
        out_shape.insert(0, jax.ShapeDtypeStruct((bt, n_c, hv), BF16))
    scratch = [pltpu.VMEM((length, dk), BF16), pltpu.VMEM((length, dk), BF16),
               pltpu.VMEM((n_c, dk), BF16), pltpu.VMEM((n_c, dk), BF16),
               pltpu.VMEM((length, dv), F32), pltpu.VMEM((n_c, dv), F32),
               pltpu.VMEM((dk, dv), F32), pltpu.VMEM((dk, dv), F32)]
    res = pl.pallas_call(
        functools.partial(_ret_kernel, need_ctx=need_ctx, chunk=RET_CHUNK),
        grid=(bt, h),
        in_specs=in_specs, out_specs=out_specs, out_shape=out_shape, scratch_shapes=scratch,
        compiler_params=_cparams("parallel", "parallel"),
        name="retention_scan",
    )(log_gamma, p_c, p_c, p_c, p_c, p_l, p_l, p_l, p_l, gn.reshape(1, hv), *tables)
    return (res[0], res[1]) if need_ctx else (None, res[0])


def _bf16_terms(x):
    hi = x.astype(BF16)
    r1 = x - hi.astype(F32)
    mid = r1.astype(BF16)
    lo = (r1 - mid.astype(F32)).astype(BF16)
    return hi, mid, lo


def _hg_kernel(qc_ref, ic_ref, gc_ref, fc_ref, ql_ref, il_ref, gl_ref, fl_ref, lb_ref, gn_ref, *rest,
               need_ctx, chunk, nblk):
    if need_ctx:
        yc_ref, yl_ref = rest[:2]
        rest = rest[2:]
    else:
        yc_ref, yl_ref = None, rest[0]
        rest = rest[1:]
    ol_s, oc_s, st_f, st_b, qd_s, kd_s, kl_s, eb_s, mn_s, tb_s, tk_s, tv_s = rest
    c = chunk
    step = pl.program_id(1)
    d = lb_ref.shape[-1]
    heads = d // HG_DK
    t = ql_ref.shape[1]
    n_lat = t // c
    n_ctx = qc_ref.shape[1] // c
    row = lax.broadcasted_iota(jnp.int32, (c, c), 0)
    col = lax.broadcasted_iota(jnp.int32, (c, c), 1)
    lower = row >= col
    upper = row <= col
    tidx = lax.broadcasted_iota(jnp.int32, (c, 1), 0)

    def gates(f, lb):
        e = jnp.exp(-jnp.abs(f))
        r = 1.0 / (1.0 + e)
        pos = f >= 0
        sig = jnp.where(pos, r, e * r)
        sig_neg = jnp.where(pos, e * r, r)
        return jnp.log(lb + (1.0 - lb) * sig), (1.0 - lb) * sig_neg

    def cumsum(mask, la):
        tri = mask.astype(BF16)
        hi, mid, lo = _bf16_terms(la)
        return _dot(tri, hi) + (_dot(tri, mid) + _dot(tri, lo))

    def prepare(q_ref, f_ref, fcols, n, mask, last):
        lb = lb_ref[...]
        mn_s[...] = jnp.zeros_like(mn_s)

        def body(i, carry):
            rows = pl.ds(pl.multiple_of(i * c, c), c)
            la, kk = gates(f_ref[0, rows, fcols], lb)
            b = cumsum(mask, la)
            b_end = b[last:last + 1, :]
            qd_s[rows, :] = (_silu(q_ref[0, rows, :].astype(F32)) * jnp.exp(b)).astype(BF16)
            kd_s[rows, :] = (kk * jnp.exp(-b)).astype(BF16)
            kl_s[rows, :] = (kk * jnp.exp(b_end - b)).astype(BF16)
            eb_s[i] = jnp.exp(b_end)
            mn_s[...] = jnp.minimum(mn_s[...], b_end)
            return carry

        lax.fori_loop(0, n, body, 0)

    def sweep(q_ref, i_ref, f_ref, fcols, n, mask, last, st, reverse, emit):
        prepare(q_ref, f_ref, fcols, n, mask, last)

        def chunk_rows(j):
            i = (n - 1 - j) if reverse else j
            return i, pl.ds(pl.multiple_of(i * c, c), c)

        def fast(_):
            def body(j, carry):
                i, rows = chunk_rows(j)
                for h in range(heads):
                    hc = slice(h * HG_DK, (h + 1) * HG_DK)
                    kl, v = kl_s[rows, hc], i_ref[0, rows, hc]
                    if emit is not None:
                        qd = qd_s[rows, hc]
                        s = jnp.where(mask, _dot_nt(qd, kd_s[rows, hc]), 0.0).astype(BF16)
                        emit(rows, hc, _dot(s, v) + _dot_nt(qd, st[h].astype(BF16)))
                    st[h] = eb_s[i, :, hc] * st[h] + _dot_tn(v, kl)
                return carry

            lax.fori_loop(0, n, body, 0)

        def exact(_):
            def body(j, carry):
                i, rows = chunk_rows(j)
                for h in range(heads):
                    hc = slice(h * HG_DK, (h + 1) * HG_DK)
                    fh = f_ref[0, rows, hc] if fcols == slice(None) else f_ref[0, rows, fcols.start + h * HG_DK:
                                                                               fcols.start + (h + 1) * HG_DK]
                    la, kk = gates(fh, lb_ref[:, hc])
                    b = cumsum(mask, la)
                    b_end = b[last:last + 1, :]
                    v = i_ref[0, rows, hc]
                    if emit is not None:
                        qs = _silu(q_ref[0, rows, hc].astype(F32))
                        tb_s[...] = b
                        tk_s[...] = kk
                        tv_s[...] = v.astype(F32)

                        def inner(s, acc):
                            one = pl.ds(s, 1)
                            dec = jnp.exp(jnp.minimum(b - tb_s[one, :], 0.0))
                            w = jnp.sum(qs * tk_s[one, :] * dec, axis=-1, keepdims=True)
                            keep = (tidx <= s) if reverse else (tidx >= s)
                            return acc + jnp.where(keep, w, 0.0) * tv_s[one, :]

                        o = lax.fori_loop(0, c, inner, jnp.zeros((c, HG_DK), F32))
                        qd = (qs * jnp.exp(b)).astype(BF16)
                        emit(rows, hc, o + _dot_nt(qd, st[h].astype(BF16)))
                    kl = (kk * jnp.exp(b_end - b)).astype(BF16)
                    st[h] = jnp.exp(b_end) * st[h] + _dot_tn(v, kl)
                return carry

            lax.fori_loop(0, n, body, 0)

        in_range = jnp.min(mn_s[...]) >= HG_SAFE_LOG_DECAY
        lax.cond(in_range, fast, exact, 0)

    def store_partial(o_s, row0):
        def emit(rows, hc, o):
            o_s[pl.ds(pl.multiple_of(row0 + rows.start, c), c), hc] = o
        return emit

    def finish(o_s, row0, g_ref, y_ref):
        def emit(rows, hc, o):
            total = o_s[pl.ds(pl.multiple_of(row0 + rows.start, c), c), hc] + o
            y = _silu(g_ref[0, rows, hc].astype(F32)) * (_rms(total) * gn_ref[:, hc])
            y_ref[0, rows, hc] = y.astype(y_ref.dtype)
        return emit

    @pl.when(step == 0)
    def _context():
        st_f[...] = jnp.zeros_like(st_f)
        st_b[...] = jnp.zeros_like(st_b)
        sweep(qc_ref, ic_ref, fc_ref, slice(0, d), n_ctx, lower, c - 1, st_f, False,
              store_partial(oc_s, 0) if need_ctx else None)
        sweep(qc_ref, ic_ref, fc_ref, slice(d, 2 * d), n_ctx, upper, 0, st_b, True,
              finish(oc_s, 0, gc_ref, yc_ref) if need_ctx else None)

    @pl.when(step < nblk)
    def _forward():
        sweep(ql_ref, il_ref, fl_ref, slice(None), n_lat, lower, c - 1, st_f, False,
              store_partial(ol_s, step * t))

    @pl.when(step >= nblk)
    def _backward():
        sweep(ql_ref, il_ref, fl_ref, slice(None), n_lat, upper, 0, st_b, True,
              finish(ol_s, (2 * nblk - 1 - step) * t, gl_ref, yl_ref))


def _hgrn2_scan(p_c, f_c, p_l, f_l, lb, gn, need_ctx):
    bt, length, d3 = p_l.shape
    d = d3 // 3
    n_c = p_c.shape[1]
    h = d // HG_DK
    t = min(HG_BLOCK, length)
    nblk = length // t
    c = HG_CHUNK

    def blk(s):
        return jnp.where(s < nblk, s, 2 * nblk - 1 - s)

    def bwd_blk(s):
        return jnp.where(s < nblk, nblk - 1, 2 * nblk - 1 - s)

    lat_specs = [pl.BlockSpec((1, t, d), lambda b, s: (b, blk(s), 0)),
                 pl.BlockSpec((1, t, d), lambda b, s: (b, blk(s), 1)),
                 pl.BlockSpec((1, t, d), lambda b, s: (b, bwd_blk(s), 2)),
                 pl.BlockSpec((1, t, d), lambda b, s: (b, blk(s), jnp.where(s < nblk, 0, 1)))]
    ctx_specs = [pl.BlockSpec((1, n_c, d), lambda b, s: (b, 0, 0)),
                 pl.BlockSpec((1, n_c, d), lambda b, s: (b, 0, 1)),
                 pl.BlockSpec((1, n_c, d), lambda b, s: (b, 0, 2)),
                 pl.BlockSpec((1, n_c, 2 * d), lambda b, s: (b, 0, 0))]
    vec = pl.BlockSpec((1, d), lambda b, s: (0, 0))
    out_specs = [pl.BlockSpec((1, t, d), lambda b, s: (b, bwd_blk(s), 0))]
    out_shape = [jax.ShapeDtypeStruct((bt, length, d), BF16)]
    if need_ctx:
        out_specs.insert(0, pl.BlockSpec((1, n_c, d), lambda b, s: (b, 0, 0)))
        out_shape.insert(0, jax.ShapeDtypeStruct((bt, n_c, d), BF16))
    rows = max(t, n_c)
    scratch = [pltpu.VMEM((length, d), F32), pltpu.VMEM((n_c, d), F32),
               pltpu.VMEM((h, HG_DK, HG_DK), F32), pltpu.VMEM((h, HG_DK, HG_DK), F32),
               pltpu.VMEM((rows, d), BF16), pltpu.VMEM((rows, d), BF16), pltpu.VMEM((rows, d), BF16),
               pltpu.VMEM((rows // c, 1, d), F32), pltpu.VMEM((1, d), F32),
               pltpu.VMEM((c, HG_DK), F32), pltpu.VMEM((c, HG_DK), F32), pltpu.VMEM((c, HG_DK), F32)]
    res = pl.pallas_call(
        functools.partial(_hg_kernel, need_ctx=need_ctx, chunk=c, nblk=nblk),
        grid=(bt, 2 * nblk),
        in_specs=ctx_specs + lat_specs + [vec, vec],
        out_specs=out_specs, out_shape=out_shape, scratch_shapes=scratch,
        compiler_params=_cparams("parallel", "arbitrary"),
        name="hgrn2_scan",
    )(p_c, p_c, p_c, f_c, p_l, p_l, p_l, f_l, lb.reshape(1, d), gn.reshape(1, d))
    return (res[0], res[1]) if need_ctx else (None, res[0])


def _split_dot(x, e):
    hi = x.astype(BF16)
    lo = (x - hi.astype(F32)).astype(BF16)
    return _dot(hi, e) + _dot(lo, e)


def _m2_kernel(zc_ref, xc_ref, bc_ref, cc_ref, dtc_ref, zl_ref, xl_ref, bl_ref, cl_ref, dtl_ref,
               cwx_ref, cwb_ref, cwc_ref, cbx_ref, cbb_ref, cbc_ref,
               sel_ref, bias_ref, aneg_ref, exp_ref, dskip_ref, gn_ref, *rest, need_ctx, chunk):
    if need_ctx:
        yc_ref, yl_ref, xl_s, bl_s, cl_s, xc_s, bc_s, cc_s, ol_s, oc_s, st_s = rest
    else:
        yl_ref, xl_s, bl_s, cl_s, xc_s, bc_s, cc_s, ol_s, oc_s, st_s = rest
        yc_ref = None
    c = chunk
    n_lat = xl_ref.shape[1] // c
    n_ctx = xc_ref.shape[1] // c
    hpg = xl_ref.shape[-1] // M2_HEADDIM
    gw = hpg * M2_HEADDIM
    row = lax.broadcasted_iota(jnp.int32, (c, c), 0)
    col = lax.broadcasted_iota(jnp.int32, (c, c), 1)
    lower = row >= col
    upper = row <= col
    tril = lower.astype(F32)
    triu = upper.astype(F32)
    ridx = lax.broadcasted_iota(jnp.int32, (c, 1), 0)

    def conv_silu(src_ref, dst_s, w_ref, b_ref, n):
        total = n * c
        halo = 16

        def step(i, carry):
            start = pl.multiple_of(i * c, c)
            rows = pl.ds(start, c)
            x = src_ref[0, rows, :].astype(F32)
            prev_blk = src_ref[0, pl.ds(pl.multiple_of(jnp.maximum(start - halo, 0), halo), halo), :]
            next_blk = src_ref[0, pl.ds(pl.multiple_of(jnp.minimum(start + c, total - halo), halo), halo), :]
            prev_row = jnp.where(i > 0, prev_blk[halo - 1:halo, :].astype(F32), 0.0)
            next_row = jnp.where(i < n - 1, next_blk[0:1, :].astype(F32), 0.0)
            x_prev = jnp.where(ridx == 0, prev_row, pltpu.roll(x, 1, 0))
            x_next = jnp.where(ridx == c - 1, next_row, pltpu.roll(x, c - 1, 0))
            y = w_ref[0:1, :] * x_prev + w_ref[1:2, :] * x + w_ref[2:3, :] * x_next + b_ref[...]
            dst_s[rows, :] = _silu(y).astype(dst_s.dtype)
            return carry

        lax.fori_loop(0, n, step, 0)

    def gates(dt_ref, rows):
        raw = jnp.dot(dt_ref[0, rows, :], sel_ref[0], precision=_HI, preferred_element_type=F32)
        x = raw + bias_ref[0]
        dt = jnp.maximum(x, 0.0) + jnp.log(1.0 + jnp.exp(-jnp.abs(x)))
        la = dt * aneg_ref[0]
        return dt, la

    def fwd_half(x_s, b_s, c_s, dt_ref, o_s, rows):
        dt, la = gates(dt_ref, rows)
        bf = jnp.dot(tril, la, precision=_HI, preferred_element_type=F32)
        bb = jnp.dot(triu, la, precision=_HI, preferred_element_type=F32)
        bf_t, bb_t, dt_t = bf.T, bb.T, dt.T
        cm, bm, xs = c_s[rows, :], b_s[rows, :], x_s[rows, :]
        base = _dot_nt(cm, bm)
        ef = _split_dot(jnp.exp(bf), exp_ref[:, :gw])
        wf = _split_dot(jnp.exp(bf[c - 1:c, :] - bf) * dt, exp_ref[:, :gw])
        sf = st_s[:, :gw]
        o_s[rows, :] += ef * _dot(cm, sf.astype(BF16))
        for r in range(hpg):
            rb = hpg + r
            hd = slice(r * M2_HEADDIM, (r + 1) * M2_HEADDIM)
            mf = jnp.where(lower, jnp.exp(jnp.minimum(bf[:, r:r + 1] - bf_t[r:r + 1, :], 0.0)) * dt_t[r:r + 1, :], 0.0)
            mb = jnp.where(upper, jnp.exp(jnp.minimum(bb[:, rb:rb + 1] - bb_t[rb:rb + 1, :], 0.0)) * dt_t[rb:rb + 1, :],
                           0.0)
            m = (base * (mf + mb)).astype(BF16)
            o_s[rows, hd] += _dot(m, x_s[rows, hd])
        xw = (xs.astype(F32) * wf).astype(BF16)
        st_s[:, :gw] = ef[c - 1:c, :] * sf + _dot_tn(bm, xw)

    def bwd_half(x_s, b_s, c_s, dt_ref, o_s, rows, with_out):
        dt, la = gates(dt_ref, rows)
        bb = jnp.dot(triu, la, precision=_HI, preferred_element_type=F32)
        cm, bm, xs = c_s[rows, :], b_s[rows, :], x_s[rows, :]
        eb = _split_dot(jnp.exp(bb), exp_ref[:, gw:])
        wb = _split_dot(jnp.exp(bb[0:1, :] - bb) * dt, exp_ref[:, gw:])
        sb = st_s[:, gw:]
        if with_out:
            o_s[rows, :] += eb * _dot(cm, sb.astype(BF16))
        xw = (xs.astype(F32) * wb).astype(BF16)
        st_s[:, gw:] = eb[0:1, :] * sb + _dot_tn(bm, xw)

    def fwd_state_only(x_s, b_s, dt_ref, rows):
        dt, la = gates(dt_ref, rows)
        bf = jnp.dot(tril, la, precision=_HI, preferred_element_type=F32)
        bm, xs = b_s[rows, :], x_s[rows, :]
        ef_end = _split_dot(jnp.exp(bf), exp_ref[:, :gw])[c - 1:c, :]
        wf = _split_dot(jnp.exp(bf[c - 1:c, :] - bf) * dt, exp_ref[:, :gw])
        xw = (xs.astype(F32) * wf).astype(BF16)
        st_s[:, :gw] = ef_end * st_s[:, :gw] + _dot_tn(bm, xw)

    def scan(x_s, b_s, c_s, dt_ref, o_s, n, with_out):
        if with_out:
            o_s[...] = jnp.zeros_like(o_s)

        def step(i, carry):
            fw = pl.ds(pl.multiple_of(i * c, c), c)
            bw = pl.ds(pl.multiple_of((n - 1 - i) * c, c), c)
            if with_out:
                fwd_half(x_s, b_s, c_s, dt_ref, o_s, fw)
            else:
                fwd_state_only(x_s, b_s, dt_ref, fw)
            bwd_half(x_s, b_s, c_s, dt_ref, o_s, bw, with_out)
            return carry

        lax.fori_loop(0, n, step, 0)

    def readout(o_s, x_s, z_ref, y_ref, n):
        def step(i, carry):
            rows = pl.ds(pl.multiple_of(i * c, c), c)
            y = o_s[rows, :] + dskip_ref[...] * x_s[rows, :].astype(F32)
            y = y * _silu(z_ref[0, rows, :].astype(F32))
            y_ref[0, rows, :] = (_rms(y) * gn_ref[...]).astype(y_ref.dtype)
            return carry

        lax.fori_loop(0, n, step, 0)

    conv_silu(xc_ref, xc_s, cwx_ref, cbx_ref, n_ctx)
    conv_silu(bc_ref, bc_s, cwb_ref, cbb_ref, n_ctx)
    conv_silu(cc_ref, cc_s, cwc_ref, cbc_ref, n_ctx)
    conv_silu(xl_ref, xl_s, cwx_ref, cbx_ref, n_lat)
    conv_silu(bl_ref, bl_s, cwb_ref, cbb_ref, n_lat)
    conv_silu(cl_ref, cl_s, cwc_ref, cbc_ref, n_lat)
    st_s[...] = jnp.zeros_like(st_s)
    scan(xc_s, bc_s, cc_s, dtc_ref, oc_s, n_ctx, need_ctx)
    if need_ctx:
        readout(oc_s, xc_s, zc_ref, yc_ref, n_ctx)
    scan(xl_s, bl_s, cl_s, dtl_ref, ol_s, n_lat, True)
    readout(ol_s, xl_s, zl_ref, yl_ref, n_lat)


def _mamba2_scan(p_c, dt_c, p_l, dt_l, conv_w, conv_b, dt_bias, a_neg, d_skip, gn, need_ctx):
    bt, length, _ = p_l.shape
    n_c = p_c.shape[1]
    dinner = gn.shape[0]
    heads = dinner // M2_HEADDIM
    g = M2_GROUPS
    hpg = heads // g
    gw = dinner // g
    ns = M2_DSTATE
    conv_dim = conv_w.shape[1]
    zb = dinner // gw
    xb = 2 * dinner // ns

    lane = jnp.arange(LANES)
    src = jnp.arange(LANES)
    grp = jnp.arange(g)
    want = jnp.where(lane < hpg, grp[:, None] * hpg + lane,
                     jnp.where(lane < 2 * hpg, heads + grp[:, None] * hpg + lane - hpg, -1))
    sel = (src[None, :, None] == want[:, None, :]).astype(F32)
    flat_bias = jnp.concatenate([dt_bias.reshape(-1).astype(F32), jnp.zeros((LANES - 2 * heads,), F32)])
    flat_aneg = jnp.concatenate([a_neg.reshape(-1).astype(F32), jnp.zeros((LANES - 2 * heads,), F32)])
    bias_sel = jnp.einsum('j,gjm->gm', flat_bias, sel, precision=_HI)[:, None, :]
    aneg_sel = jnp.einsum('j,gjm->gm', flat_aneg, sel, precision=_HI)[:, None, :]
    expand = (jnp.arange(2 * gw)[None, :] // M2_HEADDIM == jnp.arange(LANES)[:, None]).astype(BF16)
    d_exp = jnp.repeat(d_skip.astype(F32), M2_HEADDIM).reshape(1, dinner)

    def cols(rows, width, off):
        return pl.BlockSpec((1, rows, width), lambda b, j: (b, 0, off + j))

    def specs(rows):
        return [cols(rows, gw, 0), cols(rows, gw, zb), cols(rows, ns, xb), cols(rows, ns, xb + g),
                pl.BlockSpec((1, rows, LANES), lambda b, j: (b, 0, 0))]

    def cvec(nrows, width, off):
        return pl.BlockSpec((nrows, width), lambda b, j: (0, off + j))

    per_group = pl.BlockSpec((1, 1, LANES), lambda b, j: (j, 0, 0))
    in_specs = (specs(n_c) + specs(length)
                + [cvec(3, gw, 0), cvec(3, ns, dinner // ns), cvec(3, ns, dinner // ns + g),
                   cvec(1, gw, 0), cvec(1, ns, dinner // ns), cvec(1, ns, dinner // ns + g),
                   pl.BlockSpec((1, LANES, LANES), lambda b, j: (j, 0, 0)), per_group, per_group,
                   _resident(expand.shape), cvec(1, gw, 0), cvec(1, gw, 0)])
    out_specs = [cols(length, gw, 0)]
    out_shape = [jax.ShapeDtypeStruct((bt, length, dinner), BF16)]
    if need_ctx:
        out_specs.insert(0, cols(n_c, gw, 0))
        out_shape.insert(0, jax.ShapeDtypeStruct((bt, n_c, dinner), BF16))
    scratch = [pltpu.VMEM((length, gw), BF16), pltpu.VMEM((length, ns), BF16), pltpu.VMEM((length, ns), BF16),
               pltpu.VMEM((n_c, gw), BF16), pltpu.VMEM((n_c, ns), BF16), pltpu.VMEM((n_c, ns), BF16),
               pltpu.VMEM((length, gw), F32), pltpu.VMEM((n_c, gw), F32),
               pltpu.VMEM((ns, 2 * gw), F32)]
    cb = conv_b.reshape(1, conv_dim)
    res = pl.pallas_call(
        functools.partial(_m2_kernel, need_ctx=need_ctx, chunk=M2_CHUNK),
        grid=(bt, g),
        in_specs=in_specs, out_specs=out_specs, out_shape=out_shape, scratch_shapes=scratch,
        compiler_params=_cparams("parallel", "parallel"),
        name="mamba2_scan",
    )(p_c, p_c, p_c, p_c, dt_c, p_l, p_l, p_l, p_l, dt_l,
      conv_w, conv_w, conv_w, cb, cb, cb, sel, bias_sel, aneg_sel, expand, d_exp, gn.reshape(1, dinner))
    return (res[0], res[1]) if need_ctx else (None, res[0])


def kernel(x, c, ctx, c_ctx, ada_w, ada_b, norm_g, ret_w_in, ret_w_out, ret_decay, ret_gn, hg_w_in, hg_w_out, hg_lb, hg_gn, m2_w_in, m2_w_out, m2_conv_w, m2_conv_b, m2_dt_bias, m2_a_log, m2_d, m2_gn, ffn_w_up, ffn_conv_w, ffn_conv_b, ffn_w_down):
    bt, length, d = x.shape
    n_c = ctx.shape[1]
    depth = ada_w.shape[0]
    hidden = ffn_conv_w.shape[-1]
    tm = min(512, length)
    tm_c = n_c

    rows = -(-(bt + 1) // 8) * 8
    cc = jnp.concatenate([c, c_ctx[None, :], jnp.zeros((rows - bt - 1, d), F32)], axis=0)
    mod = _modulation(cc, ada_w, ada_b)

    lb_cum = jnp.cumsum(jax.nn.softmax(hg_lb.astype(F32), axis=0), axis=0)
    lb_all = lb_cum - lb_cum[0]

    h_l, h_c = x, ctx
    for i in range(depth):
        kind, j = i % N_MIXERS, i // N_MIXERS
        need_ctx = i < depth - 1
        mod_l = mod[i, :bt].reshape(bt, 6, 1, d)
        mod_c = mod[i, bt].reshape(1, 6, 1, d)
        sh1_l, sc1_l, g1_l, sh2_l, sc2_l, g2_l = (mod_l[:, k] for k in range(6))
        sh1_c, sc1_c, g1_c, sh2_c, sc2_c, g2_c = (mod_c[:, k] for k in range(6))

        if kind == 0:
            w_in = ret_w_in[j].astype(BF16)
            p_l = _project(h_l, norm_g[i, 0], sc1_l, sh1_l, w_in, tm=tm)
            p_c = _project(h_c, norm_g[i, 0], sc1_c, sh1_c, w_in, tm=tm_c)
            log_gamma = jax.nn.log_sigmoid(ret_decay[j].astype(F32))
            y_c, y_l = _retention_scan(p_c, p_l, log_gamma, ret_gn[j], need_ctx)
            w_out = ret_w_out[j].astype(BF16)
        elif kind == 1:
            wq, wff, wfb, wi, wg = jnp.split(hg_w_in[j], 5, axis=-1)
            w_main = jnp.concatenate([wq, wi, wg], axis=-1).astype(BF16)
            w_gate = jnp.concatenate([wff, wfb], axis=-1).astype(BF16)
            p_l, f_l = _project(h_l, norm_g[i, 0], sc1_l, sh1_l, w_main, w_gate, tm=tm)
            p_c, f_c = _project(h_c, norm_g[i, 0], sc1_c, sh1_c, w_main, w_gate, tm=tm_c)
            y_c, y_l = _hgrn2_scan(p_c, f_c, p_l, f_l, lb_all[i], hg_gn[j], need_ctx)
            w_out = hg_w_out[j].astype(BF16)
        else:
            dinner = m2_gn.shape[-1]
            n_main = dinner + m2_conv_w.shape[-1]
            w_main = m2_w_in[j][:, :n_main].astype(BF16)
            w_dt = m2_w_in[j][:, n_main:]
            w_dt = jnp.concatenate([w_dt, jnp.zeros((d, LANES - w_dt.shape[1]), w_dt.dtype)], axis=-1).astype(BF16)
            p_l, dt_l = _project(h_l, norm_g[i, 0], sc1_l, sh1_l, w_main, w_dt, tm=tm)
            p_c, dt_c = _project(h_c, norm_g[i, 0], sc1_c, sh1_c, w_main, w_dt, tm=tm_c)
            a_neg = -jnp.exp(m2_a_log[j].astype(F32))
            y_c, y_l = _mamba2_scan(p_c, dt_c, p_l, dt_l, m2_conv_w[j], m2_conv_b[j], m2_dt_bias[j], a_neg,
                                    m2_d[j], m2_gn[j], need_ctx)
            w_out = m2_w_out[j].astype(BF16)

        wa = ffn_w_up[i][:, :hidden].astype(BF16)
        wv = ffn_w_up[i][:, hidden:].astype(BF16)
        wd = ffn_w_down[i].astype(BF16)
        h_l = _out_project(y_l, w_out, h_l, norm_g[i, 1], g1_l, tm=tm)
        h_l = _conv_ffn(h_l, norm_g[i, 2], sc2_l, sh2_l, wa, wv, ffn_conv_w[i], ffn_conv_b[i], wd,
                        norm_g[i, 3], g2_l, tm=tm, seg=GRID_W)
        if need_ctx:
            h_c = _out_project(y_c, w_out, h_c, norm_g[i, 1], g1_c, tm=tm_c)
            h_c = _conv_ffn(h_c, norm_g[i, 2], sc2_c, sh2_c, wa, wv, ffn_conv_w[i], ffn_conv_b[i], wd,
                            norm_g[i, 3], g2_c, tm=tm_c, seg=n_c)
    return h_l
```

```python
import functools
import math

import jax
import jax.numpy as jnp
from jax import lax
from jax.experimental import pallas as pl
from jax.experimental.pallas import tpu as pltpu

F32 = jnp.float32
BF16 = jnp.bfloat16

NORM_EPS = 1e-6
GRID_W = 64
ROPE_BASE = 10000.0
N_MIXERS = 3

RET_HEADS = 4
RET_CHUNK = 256
HG_DK = 128
HG_CHUNK = 64
HG_BLOCK = 512
HG_SAFE_LOG_DECAY = -80.0
M2_HEADDIM = 64
M2_GROUPS = 4
M2_DSTATE = 128
M2_CHUNK = 128
LANES = 128

V7X_VMEM_LIMIT_BYTES = 56 * 1024 * 1024

_HI = lax.Precision.HIGHEST


def _cparams(*sem):
    return pltpu.CompilerParams(dimension_semantics=sem, vmem_limit_bytes=V7X_VMEM_LIMIT_BYTES)


def _sigmoid(x):
    return 1.0 / (1.0 + jnp.exp(-x))


def _silu(x):
    return x * _sigmoid(x)


def _rms(t):
    return t * lax.rsqrt(jnp.mean(t * t, axis=-1, keepdims=True) + NORM_EPS)


def _dot(a, b):
    return jnp.dot(a, b, preferred_element_type=F32)


def _dot_nt(a, b):
    return lax.dot_general(a, b, (((1,), (1,)), ((), ())), preferred_element_type=F32)


def _dot_tn(a, b):
    return lax.dot_general(a, b, (((0,), (0,)), ((), ())), preferred_element_type=F32)


def _resident(shape):
    return pl.BlockSpec(shape, lambda *_: (0,) * len(shape), pipeline_mode=pl.Buffered(1))


def _mod_kernel(c_ref, w_ref, b_ref, o_ref):
    s = _silu(c_ref[...]).astype(BF16)
    o_ref[0] = _dot(s, w_ref[0].astype(BF16)) + b_ref[0]


def _modulation(cc, ada_w, ada_b):
    depth, d, n = ada_w.shape
    rows = cc.shape[0]
    tn = 1536
    return pl.pallas_call(
        _mod_kernel,
        grid=(depth, n // tn),
        in_specs=[pl.BlockSpec((rows, d), lambda i, j: (0, 0)),
                  pl.BlockSpec((1, d, tn), lambda i, j: (i, 0, j)),
                  pl.BlockSpec((1, 1, tn), lambda i, j: (i, 0, j))],
        out_specs=pl.BlockSpec((1, rows, tn), lambda i, j: (i, 0, j)),
        out_shape=jax.ShapeDtypeStruct((depth, rows, n), F32),
        compiler_params=_cparams("parallel", "parallel"),
        name="adaln_mod",
    )(cc, ada_w, ada_b.reshape(depth, 1, n))


def _norm_mod(x, g, sc, sh):
    return _rms(x) * (g * (1.0 + sc)) + sh


def _proj_kernel(x_ref, g_ref, sc_ref, sh_ref, w_ref, *rest, ncol):
    u = _norm_mod(x_ref[0], g_ref[...], sc_ref[0], sh_ref[0]).astype(BF16)
    if len(rest) == 3:
        w2_ref, o_ref, o2_ref = rest
        o2_ref[0] = _dot(u, w2_ref[...])
    else:
        (o_ref,) = rest
    n = w_ref.shape[1]
    for n0 in range(0, n, ncol):
        n1 = min(n0 + ncol, n)
        o_ref[0, :, n0:n1] = _dot(u, w_ref[:, n0:n1]).astype(o_ref.dtype)


def _project(h, g, sc, sh, w, w2=None, *, tm):
    bt, length, d = h.shape
    n = w.shape[1]
    per_batch = sc.shape[0] != 1
    mod_spec = pl.BlockSpec((1, 1, d), (lambda b, i: (b, 0, 0)) if per_batch else (lambda b, i: (0, 0, 0)))
    in_specs = [pl.BlockSpec((1, tm, d), lambda b, i: (b, i, 0)),
                pl.BlockSpec((1, d), lambda b, i: (0, 0)),
                mod_spec, mod_spec, _resident((d, n))]
    out_specs = [pl.BlockSpec((1, tm, n), lambda b, i: (b, i, 0))]
    out_shape = [jax.ShapeDtypeStruct((bt, length, n), BF16)]
    args = [h, g.reshape(1, d), sc, sh, w]
    if w2 is not None:
        n2 = w2.shape[1]
        in_specs.append(_resident((d, n2)))
        out_specs.append(pl.BlockSpec((1, tm, n2), lambda b, i: (b, i, 0)))
        out_shape.append(jax.ShapeDtypeStruct((bt, length, n2), F32))
        args.append(w2)
    res = pl.pallas_call(
        functools.partial(_proj_kernel, ncol=512),
        grid=(bt, length // tm),
        in_specs=in_specs, out_specs=out_specs, out_shape=out_shape,
        compiler_params=_cparams("parallel", "parallel"),
        name="norm_mod_proj",
    )(*args)
    return res if w2 is not None else res[0]


def _out_kernel(y_ref, w_ref, h_ref, ng_ref, gate_ref, o_ref):
    t = _dot(y_ref[0], w_ref[...])
    o_ref[0] = h_ref[0] + gate_ref[0] * (_rms(t) * ng_ref[...])


def _out_project(y, w, h, ng, gate, *, tm):
    bt, length, d = h.shape
    dy = y.shape[-1]
    per_batch = gate.shape[0] != 1
    gate_spec = pl.BlockSpec((1, 1, d), (lambda b, i: (b, 0, 0)) if per_batch else (lambda b, i: (0, 0, 0)))
    return pl.pallas_call(
        _out_kernel,
        grid=(bt, length // tm),
        in_specs=[pl.BlockSpec((1, tm, dy), lambda b, i: (b, i, 0)),
                  _resident((dy, d)),
                  pl.BlockSpec((1, tm, d), lambda b, i: (b, i, 0)),
                  pl.BlockSpec((1, d), lambda b, i: (0, 0)),
                  gate_spec],
        out_specs=pl.BlockSpec((1, tm, d), lambda b, i: (b, i, 0)),
        out_shape=jax.ShapeDtypeStruct((bt, length, d), F32),
        compiler_params=_cparams("parallel", "parallel"),
        name="out_proj_residual",
    )(y, w, h, ng.reshape(1, d), gate)


def _gelu_tanh(x):
    return 0.5 * x * (1.0 + jnp.tanh(math.sqrt(2.0 / math.pi) * (x + 0.044715 * (x * x * x))))


def _ffn_kernel(h_ref, g_ref, sc_ref, sh_ref, wa_ref, wv_ref, cw_ref, cb_ref, wd_ref, ng_ref, gate_ref, o_ref,
                *, seg, hidden_chunks):
    h = h_ref[0]
    tm = h.shape[0]
    u = _norm_mod(h, g_ref[...], sc_ref[0], sh_ref[0]).astype(BF16)
    pos = lax.broadcasted_iota(jnp.int32, (tm, 1), 0) % seg
    has_prev = pos != 0
    has_next = pos != seg - 1
    acc = None
    for c0, c1 in hidden_chunks:
        a = _dot(u, wa_ref[:, c0:c1])
        v = _dot(u, wv_ref[:, c0:c1])
        a_prev = jnp.where(has_prev, pltpu.roll(a, 1, 0), 0.0)
        a_next = jnp.where(has_next, pltpu.roll(a, tm - 1, 0), 0.0)
        ac = (cw_ref[0:1, c0:c1] * a_prev + cw_ref[1:2, c0:c1] * a + cw_ref[2:3, c0:c1] * a_next
              + cb_ref[:, c0:c1])
        hid = (_gelu_tanh(ac) * v).astype(BF16)
        part = _dot(hid, wd_ref[c0:c1, :])
        acc = part if acc is None else acc + part
    o_ref[0] = h + gate_ref[0] * (_rms(acc) * ng_ref[...])


def _hidden_chunks(hidden, size):
    return tuple((c0, min(c0 + size, hidden)) for c0 in range(0, hidden, size))


def _conv_ffn(h, g, sc, sh, wa, wv, cw, cb, wd, ng, gate, *, tm, seg):
    bt, length, d = h.shape
    hidden = wa.shape[1]
    per_batch = sc.shape[0] != 1
    mod_spec = pl.BlockSpec((1, 1, d), (lambda b, i: (b, 0, 0)) if per_batch else (lambda b, i: (0, 0, 0)))
    vec_d = pl.BlockSpec((1, d), lambda b, i: (0, 0))
    return pl.pallas_call(
        functools.partial(_ffn_kernel, seg=seg, hidden_chunks=_hidden_chunks(hidden, 1024)),
        grid=(bt, length // tm),
        in_specs=[pl.BlockSpec((1, tm, d), lambda b, i: (b, i, 0)),
                  vec_d, mod_spec, mod_spec,
                  _resident((d, hidden)), _resident((d, hidden)),
                  pl.BlockSpec((3, hidden), lambda b, i: (0, 0)),
                  pl.BlockSpec((1, hidden), lambda b, i: (0, 0)),
                  _resident((hidden, d)),
                  vec_d, mod_spec],
        out_specs=pl.BlockSpec((1, tm, d), lambda b, i: (b, i, 0)),
        out_shape=jax.ShapeDtypeStruct((bt, length, d), F32),
        compiler_params=_cparams("parallel", "parallel"),
        name="conv_glu_ffn",
    )(h, g.reshape(1, d), sc, sh, wa, wv, cw, cb.reshape(1, hidden), wd, ng.reshape(1, d), gate)


def _ret_kernel(lg_ref, qc_ref, kc_ref, vc_ref, gc_ref, ql_ref, kl_ref, vl_ref, gl_ref, gn_ref,
                rcos_ref, rsin_ref, ccos_ref, csin_ref, *rest, need_ctx, chunk):
    if need_ctx:
        yc_ref, yl_ref, qr_s, kr_s, qcs_s, kcs_s, ol_s, oc_s, sf_s, sb_s = rest
    else:
        yl_ref, qr_s, kr_s, qcs_s, kcs_s, ol_s, oc_s, sf_s, sb_s = rest
        yc_ref = None
    c = chunk
    head = pl.program_id(1)
    lgf = lg_ref[0, head]
    lgb = lg_ref[1, head]
    dk = ql_ref.shape[-1]
    k_scale = dk ** -0.5
    n_lat = ql_ref.shape[1] // c
    n_ctx = qc_ref.shape[1] // c

    ti = lax.broadcasted_iota(jnp.int32, (c, 1), 0).astype(F32)
    dq_f = jnp.exp(lgf * (ti + 1.0))
    dk_f = jnp.exp(lgf * (c - 1.0 - ti))
    dq_b = jnp.exp(lgb * (c - ti))
    dk_b = jnp.exp(lgb * ti)
    full_chunk = jnp.full((1, 1), float(c), F32)
    dc_f = jnp.exp(lgf * full_chunk)
    dc_b = jnp.exp(lgb * full_chunk)
    diff = (lax.broadcasted_iota(jnp.int32, (c, c), 0) - lax.broadcasted_iota(jnp.int32, (c, c), 1)).astype(F32)
    dmask = (jnp.where(diff >= 0, jnp.exp(lgf * jnp.maximum(diff, 0.0)), 0.0)
             + jnp.where(diff <= 0, jnp.exp(lgb * jnp.maximum(-diff, 0.0)), 0.0))

    half = dk // 2

    def rope_row(r, carry):
        rows = pl.ds(pl.multiple_of(r * GRID_W, GRID_W), GRID_W)
        cr = rcos_ref[pl.ds(r, 1), :]
        sr = rsin_ref[pl.ds(r, 1), :]
        for src, dst, scale in ((ql_ref, qr_s, 1.0), (kl_ref, kr_s, k_scale)):
            t = src[0, rows, :].astype(F32) * scale
            t0, t1 = t[:, :half], t[:, half:]
            dst[rows, :half] = (t0 * cr + pltpu.roll(t0, half // 2, 1) * sr).astype(BF16)
            dst[rows, half:] = (t1 * ccos_ref[...] + pltpu.roll(t1, half // 2, 1) * csin_ref[...]).astype(BF16)
        return carry

    lax.fori_loop(0, ql_ref.shape[1] // GRID_W, rope_row, 0, unroll=4)
    qcs_s[...] = qc_ref[0]
    kcs_s[...] = (kc_ref[0].astype(F32) * k_scale).astype(BF16)

    sf_s[...] = jnp.zeros_like(sf_s)
    sb_s[...] = jnp.zeros_like(sb_s)

    def scan(q_s, k_s, v_ref, o_s, n, with_out):
        if with_out:
            o_s[...] = jnp.zeros_like(o_s)

        def step(i, carry):
            fw = pl.ds(pl.multiple_of(i * c, c), c)
            bw = pl.ds(pl.multiple_of((n - 1 - i) * c, c), c)
            qf, kf, vf = q_s[fw, :], k_s[fw, :], v_ref[0, fw, :]
            if with_out:
                p = (_dot_nt(qf, kf) * dmask).astype(BF16)
                o_s[fw, :] += _dot(p, vf) + dq_f * _dot(qf, sf_s[...].astype(BF16))
            kd = (kf.astype(F32) * dk_f).astype(BF16)
            sf_s[...] = dc_f * sf_s[...] + _dot_tn(kd, vf)
            qb, kb, vb = q_s[bw, :], k_s[bw, :], v_ref[0, bw, :]
            if with_out:
                o_s[bw, :] += dq_b * _dot(qb, sb_s[...].astype(BF16))
            kdb = (kb.astype(F32) * dk_b).astype(BF16)
            sb_s[...] = dc_b * sb_s[...] + _dot_tn(kdb, vb)
            return carry

        lax.fori_loop(0, n, step, 0)

    def readout(o_s, g_ref, y_ref, n):
        def step(i, carry):
            rows = pl.ds(pl.multiple_of(i * c, c), c)
            o = o_s[rows, :]
            o = o - jnp.mean(o, axis=-1, keepdims=True)
            o = _rms(o) * gn_ref[...]
            y_ref[0, rows, :] = (_silu(g_ref[0, rows, :].astype(F32)) * o).astype(y_ref.dtype)
            return carry

        lax.fori_loop(0, n, step, 0)

    scan(qcs_s, kcs_s, vc_ref, oc_s, n_ctx, need_ctx)
    if need_ctx:
        readout(oc_s, gc_ref, yc_ref, n_ctx)
    scan(qr_s, kr_s, vl_ref, ol_s, n_lat, True)
    readout(ol_s, gl_ref, yl_ref, n_lat)


def _rope_tables(length, dk):
    nf = dk // 4
    inv = ROPE_BASE ** (-jnp.arange(nf, dtype=F32) / nf)

    def tab(p):
        ang = p[:, None] * inv
        cos, sin = jnp.cos(ang), jnp.sin(ang)
        return jnp.concatenate([cos, cos], -1), jnp.concatenate([-sin, sin], -1)

    rcos, rsin = tab(jnp.arange(length // GRID_W).astype(F32))
    ccos, csin = tab(jnp.arange(GRID_W).astype(F32))
    return rcos, rsin, ccos, csin


def _retention_scan(p_c, p_l, log_gamma, gn, need_ctx):
    bt, length, _ = p_l.shape
    n_c = p_c.shape[1]
    hv = gn.shape[0]
    dv = hv // RET_HEADS
    dk = dv // 2
    h = RET_HEADS
    tables = _rope_tables(length, dk)

    def cols(rows, width, off):
        return pl.BlockSpec((1, rows, width), lambda b, j: (b, 0, off + j))

    def specs(rows):
        return [cols(rows, dk, 0), cols(rows, dk, h), cols(rows, dv, h), cols(rows, dv, 2 * h)]

    in_specs = ([pl.BlockSpec(memory_space=pltpu.SMEM)] + specs(n_c) + specs(length)
                + [pl.BlockSpec((1, dv), lambda b, j: (0, j))]
                + [_resident(t.shape) for t in tables])
    out_specs = [cols(length, dv, 0)]
    out_shape = [jax.ShapeDtypeStruct((bt, length, hv), BF16)]
    if need_ctx:
        out_specs.insert(0, cols(n_c, dv, 0))
        out_shape.insert(0, jax.ShapeDtypeStruct((bt, n_c, hv), BF16))
    scratch = [pltpu.VMEM((length, dk), BF16), pltpu.VMEM((length, dk), BF16),
               pltpu.VMEM((n_c, dk), BF16), pltpu.VMEM((n_c, dk), BF16),
               pltpu.VMEM((length, dv), F32), pltpu.VMEM((n_c, dv), F32),
               pltpu.VMEM((dk, dv), F32), pltpu.VMEM((dk, dv), F32)]
    res = pl.pallas_call(
        functools.partial(_ret_kernel, need_ctx=need_ctx, chunk=RET_CHUNK),
        grid=(bt, h),
        in_specs=in_specs, out_specs=out_specs, out_shape=out_shape, scratch_shapes=scratch,
        compiler_params=_cparams("parallel", "parallel"),
        name="retention_scan",
    )(log_gamma, p_c, p_c, p_c, p_c, p_l, p_l, p_l, p_l, gn.reshape(1, hv), *tables)
    return (res[0], res[1]) if need_ctx else (None, res[0])


def _bf16_terms(x):
    hi = x.astype(BF16)
    r1 = x - hi.astype(F32)
    mid = r1.astype(BF16)
    lo = (r1 - mid.astype(F32)).astype(BF16)
    return hi, mid, lo


def _hg_kernel(qc_ref, ic_ref, gc_ref, fc_ref, ql_ref, il_ref, gl_ref, fl_ref, lb_ref, gn_ref, *rest,
               need_ctx, chunk, nblk):
    if need_ctx:
        yc_ref, yl_ref = rest[:2]
        rest = rest[2:]
    else:
        yc_ref, yl_ref = None, rest[0]
        rest = rest[1:]
    ol_s, oc_s, st_f, st_b, qd_s, kd_s, kl_s, eb_s, mn_s, tb_s, tk_s, tv_s = rest
    c = chunk
    step = pl.program_id(1)
    d = lb_ref.shape[-1]
    heads = d // HG_DK
    t = ql_ref.shape[1]
    n_lat = t // c
    n_ctx = qc_ref.shape[1] // c
    row = lax.broadcasted_iota(jnp.int32, (c, c), 0)
    col = lax.broadcasted_iota(jnp.int32, (c, c), 1)
    lower = row >= col
    upper = row <= col
    tidx = lax.broadcasted_iota(jnp.int32, (c, 1), 0)

    def gates(f, lb):
        e = jnp.exp(-jnp.abs(f))
        r = 1.0 / (1.0 + e)
        pos = f >= 0
        sig = jnp.where(pos, r, e * r)
        sig_neg = jnp.where(pos, e * r, r)
        return jnp.log(lb + (1.0 - lb) * sig), (1.0 - lb) * sig_neg

    def cumsum(mask, la):
        tri = mask.astype(BF16)
        hi, mid, lo = _bf16_terms(la)
        return _dot(tri, hi) + (_dot(tri, mid) + _dot(tri, lo))

    def prepare(q_ref, f_ref, fcols, n, mask, last):
        lb = lb_ref[...]
        mn_s[...] = jnp.zeros_like(mn_s)

        def body(i, carry):
            rows = pl.ds(pl.multiple_of(i * c, c), c)
            la, kk = gates(f_ref[0, rows, fcols], lb)
            b = cumsum(mask, la)
            b_end = b[last:last + 1, :]
            qd_s[rows, :] = (_silu(q_ref[0, rows, :].astype(F32)) * jnp.exp(b)).astype(BF16)
            kd_s[rows, :] = (kk * jnp.exp(-b)).astype(BF16)
            kl_s[rows, :] = (kk * jnp.exp(b_end - b)).astype(BF16)
            eb_s[i] = jnp.exp(b_end)
            mn_s[...] = jnp.minimum(mn_s[...], b_end)
            return carry

        lax.fori_loop(0, n, body, 0)

    def sweep(q_ref, i_ref, f_ref, fcols, n, mask, last, st, reverse, emit):
        prepare(q_ref, f_ref, fcols, n, mask, last)

        def chunk_rows(j):
            i = (n - 1 - j) if reverse else j
            return i, pl.ds(pl.multiple_of(i * c, c), c)

        def fast(_):
            def body(j, carry):
                i, rows = chunk_rows(j)
                for h in range(heads):
                    hc = slice(h * HG_DK, (h + 1) * HG_DK)
                    kl, v = kl_s[rows, hc], i_ref[0, rows, hc]
                    if emit is not None:
                        qd = qd_s[rows, hc]
                        s = jnp.where(mask, _dot_nt(qd, kd_s[rows, hc]), 0.0).astype(BF16)
                        emit(rows, hc, _dot(s, v) + _dot_nt(qd, st[h].astype(BF16)))
                    st[h] = eb_s[i, :, hc] * st[h] + _dot_tn(v, kl)
                return carry

            lax.fori_loop(0, n, body, 0)

        def exact(_):
            def body(j, carry):
                i, rows = chunk_rows(j)
                for h in range(heads):
                    hc = slice(h * HG_DK, (h + 1) * HG_DK)
                    fh = f_ref[0, rows, hc] if fcols == slice(None) else f_ref[0, rows, fcols.start + h * HG_DK:
                                                                               fcols.start + (h + 1) * HG_DK]
                    la, kk = gates(fh, lb_ref[:, hc])
                    b = cumsum(mask, la)
                    b_end = b[last:last + 1, :]
                    v = i_ref[0, rows, hc]
                    if emit is not None:
                        qs = _silu(q_ref[0, rows, hc].astype(F32))
                        tb_s[...] = b
                        tk_s[...] = kk
                        tv_s[...] = v.astype(F32)

                        def inner(s, acc):
                            one = pl.ds(s, 1)
                            dec = jnp.exp(jnp.minimum(b - tb_s[one, :], 0.0))
                            w = jnp.sum(qs * tk_s[one, :] * dec, axis=-1, keepdims=True)
                            keep = (tidx <= s) if reverse else (tidx >= s)
                            return acc + jnp.where(keep, w, 0.0) * tv_s[one, :]

                        o = lax.fori_loop(0, c, inner, jnp.zeros((c, HG_DK), F32))
                        qd = (qs * jnp.exp(b)).astype(BF16)
                        emit(rows, hc, o + _dot_nt(qd, st[h].astype(BF16)))
                    kl = (kk * jnp.exp(b_end - b)).astype(BF16)
                    st[h] = jnp.exp(b_end) * st[h] + _dot_tn(v, kl)
                return carry

            lax.fori_loop(0, n, body, 0)

        in_range = jnp.min(mn_s[...]) >= HG_SAFE_LOG_DECAY
        lax.cond(in_range, fast, exact, 0)

    def store_partial(o_s, row0):
        def emit(rows, hc, o):
            o_s[pl.ds(pl.multiple_of(row0 + rows.start, c), c), hc] = o
        return emit

    def finish(o_s, row0, g_ref, y_ref):
        def emit(rows, hc, o):
            total = o_s[pl.ds(pl.multiple_of(row0 + rows.start, c), c), hc] + o
            y = _silu(g_ref[0, rows, hc].astype(F32)) * (_rms(total) * gn_ref[:, hc])
            y_ref[0, rows, hc] = y.astype(y_ref.dtype)
        return emit

    @pl.when(step == 0)
    def _context():
        st_f[...] = jnp.zeros_like(st_f)
        st_b[...] = jnp.zeros_like(st_b)
        sweep(qc_ref, ic_ref, fc_ref, slice(0, d), n_ctx, lower, c - 1, st_f, False,
              store_partial(oc_s, 0) if need_ctx else None)
        sweep(qc_ref, ic_ref, fc_ref, slice(d, 2 * d), n_ctx, upper, 0, st_b, True,
              finish(oc_s, 0, gc_ref, yc_ref) if need_ctx else None)

    @pl.when(step < nblk)
    def _forward():
        sweep(ql_ref, il_ref, fl_ref, slice(None), n_lat, lower, c - 1, st_f, False,
              store_partial(ol_s, step * t))

    @pl.when(step >= nblk)
    def _backward():
        sweep(ql_ref, il_ref, fl_ref, slice(None), n_lat, upper, 0, st_b, True,
              finish(ol_s, (2 * nblk - 1 - step) * t, gl_ref, yl_ref))


def _hgrn2_scan(p_c, f_c, p_l, f_l, lb, gn, need_ctx):
    bt, length, d3 = p_l.shape
    d = d3 // 3
    n_c = p_c.shape[1]
    h = d // HG_DK
    t = min(HG_BLOCK, length)
    nblk = length // t
    c = HG_CHUNK

    def blk(s):
        return jnp.where(s < nblk, s, 2 * nblk - 1 - s)

    def bwd_blk(s):
        return jnp.where(s < nblk, nblk - 1, 2 * nblk - 1 - s)

    lat_specs = [pl.BlockSpec((1, t, d), lambda b, s: (b, blk(s), 0)),
                 pl.BlockSpec((1, t, d), lambda b, s: (b, blk(s), 1)),
                 pl.BlockSpec((1, t, d), lambda b, s: (b, bwd_blk(s), 2)),
                 pl.BlockSpec((1, t, d), lambda b, s: (b, blk(s), jnp.where(s < nblk, 0, 1)))]
    ctx_specs = [pl.BlockSpec((1, n_c, d), lambda b, s: (b, 0, 0)),
                 pl.BlockSpec((1, n_c, d), lambda b, s: (b, 0, 1)),
                 pl.BlockSpec((1, n_c, d), lambda b, s: (b, 0, 2)),
                 pl.BlockSpec((1, n_c, 2 * d), lambda b, s: (b, 0, 0))]
    vec = pl.BlockSpec((1, d), lambda b, s: (0, 0))
    out_specs = [pl.BlockSpec((1, t, d), lambda b, s: (b, bwd_blk(s), 0))]
    out_shape = [jax.ShapeDtypeStruct((bt, length, d), BF16)]
    if need_ctx:
        out_specs.insert(0, pl.BlockSpec((1, n_c, d), lambda b, s: (b, 0, 0)))
        out_shape.insert(0, jax.ShapeDtypeStruct((bt, n_c, d), BF16))
    rows = max(t, n_c)
    scratch = [pltpu.VMEM((length, d), F32), pltpu.VMEM((n_c, d), F32),
               pltpu.VMEM((h, HG_DK, HG_DK), F32), pltpu.VMEM((h, HG_DK, HG_DK), F32),
               pltpu.VMEM((rows, d), BF16), pltpu.VMEM((rows, d), BF16), pltpu.VMEM((rows, d), BF16),
               pltpu.VMEM((rows // c, 1, d), F32), pltpu.VMEM((1, d), F32),
               pltpu.VMEM((c, HG_DK), F32), pltpu.VMEM((c, HG_DK), F32), pltpu.VMEM((c, HG_DK), F32)]
    res = pl.pallas_call(
        functools.partial(_hg_kernel, need_ctx=need_ctx, chunk=c, nblk=nblk),
        grid=(bt, 2 * nblk),
        in_specs=ctx_specs + lat_specs + [vec, vec],
        out_specs=out_specs, out_shape=out_shape, scratch_shapes=scratch,
        compiler_params=_cparams("parallel", "arbitrary"),
        name="hgrn2_scan",
    )(p_c, p_c, p_c, f_c, p_l, p_l, p_l, f_l, lb.reshape(1, d), gn.reshape(1, d))
    return (res[0], res[1]) if need_ctx else (None, res[0])


def _dot_terms(x, m):
    hi, mid, lo = _bf16_terms(x)
    return _dot(hi, m) + (_dot(mid, m) + _dot(lo, m))


def _m2_kernel(zc_ref, xc_ref, bc_ref, cc_ref, dtc_ref, zl_ref, xl_ref, bl_ref, cl_ref, dtl_ref,
               cwx_ref, cwb_ref, cwc_ref, cbx_ref, cbb_ref, cbc_ref,
               bias_ref, aneg_ref, ex_ref, ede_ref, dm_ref, dskip_ref, gn_ref, *rest, need_ctx, chunk):
    if need_ctx:
        yc_ref, yl_ref = rest[:2]
        rest = rest[2:]
    else:
        yc_ref, yl_ref = None, rest[0]
        rest = rest[1:]
    xl_s, bl_s, cl_s, xc_s, bc_s, cc_s, ol_s, oc_s, st_s, p_s, dtt_s, nb_s, de_s = rest
    c = chunk
    n_lat = xl_ref.shape[1] // c
    n_ctx = xc_ref.shape[1] // c
    hpg = xl_ref.shape[-1] // M2_HEADDIM
    gw = hpg * M2_HEADDIM
    row = lax.broadcasted_iota(jnp.int32, (c, c), 0)
    col = lax.broadcasted_iota(jnp.int32, (c, c), 1)
    lower = row >= col
    upper = row <= col
    tril = lower.astype(BF16)
    triu = upper.astype(BF16)
    ridx = lax.broadcasted_iota(jnp.int32, (c, 1), 0)
    lane = lax.broadcasted_iota(jnp.int32, (1, LANES), 1)
    fwd_lane = lane < hpg
    nd = 2 * hpg
    l_mid, l_lo, l_dt, l_e, l_w, l_one = (k * nd for k in range(1, 7))

    def conv_silu(src_ref, dst_s, w_ref, b_ref, n):
        total = n * c
        halo = 16

        def step(i, carry):
            start = pl.multiple_of(i * c, c)
            rows = pl.ds(start, c)
            x = src_ref[0, rows, :].astype(F32)
            prev_blk = src_ref[0, pl.ds(pl.multiple_of(jnp.maximum(start - halo, 0), halo), halo), :]
            next_blk = src_ref[0, pl.ds(pl.multiple_of(jnp.minimum(start + c, total - halo), halo), halo), :]
            prev_row = jnp.where(i > 0, prev_blk[halo - 1:halo, :].astype(F32), 0.0)
            next_row = jnp.where(i < n - 1, next_blk[0:1, :].astype(F32), 0.0)
            x_prev = jnp.where(ridx == 0, prev_row, pltpu.roll(x, 1, 0))
            x_next = jnp.where(ridx == c - 1, next_row, pltpu.roll(x, c - 1, 0))
            y = w_ref[0:1, :] * x_prev + w_ref[1:2, :] * x + w_ref[2:3, :] * x_next + b_ref[...]
            dst_s[rows, :] = _silu(y).astype(dst_s.dtype)
            return carry

        lax.fori_loop(0, n, step, 0)

    def prepare(dt_ref, n):
        def step(i, carry):
            rows = pl.ds(pl.multiple_of(i * c, c), c)
            x = dt_ref[0, rows, :] + bias_ref[0]
            dt = jnp.maximum(x, 0.0) + jnp.log(1.0 + jnp.exp(-jnp.abs(x)))
            hi, mid, lo = _bf16_terms(dt * aneg_ref[0])
            bf = _dot(tril, hi) + (_dot(tril, mid) + _dot(tril, lo))
            bb = _dot(triu, hi) + (_dot(triu, mid) + _dot(triu, lo))
            b = jnp.where(fwd_lane, bf, bb)
            b_end = jnp.where(fwd_lane, bf[c - 1:c, :], bb[0:1, :])
            w = jnp.exp(b_end - b) * dt
            hi, mid, lo = (t.astype(F32) for t in _bf16_terms(b))
            q = jnp.where(lane < l_one + 3, 1.0, 0.0)
            for first, field in ((l_w, w), (l_e, jnp.exp(b)), (l_dt, dt), (l_lo, lo), (l_mid, mid)):
                q = jnp.where(lane < first + nd, pltpu.roll(field, first, 1), q)
            q = jnp.where(lane < nd, hi, q)
            p_s[rows, :] = q.astype(BF16)
            qt = q.T
            dtt_s[i] = qt[l_dt:l_dt + nd, :]
            neg = [jnp.concatenate([-qt[first + r:first + r + 1, :] for r in range(nd)], axis=1)
                   for first in (0, l_mid, l_lo)]
            nb_s[i] = jnp.concatenate(neg + [jnp.zeros((nd - 3, nd * c), F32)], axis=0).astype(BF16)
            de_s[i] = _dot_terms(jnp.broadcast_to(jnp.exp(b_end), (8, LANES)), ede_ref[...])[0:1, :]
            return carry

        lax.fori_loop(0, n, step, 0, unroll=min(4, n))

    def forward(x_s, b_s, c_s, o_s, n, with_out):
        def step(i, carry):
            rows = pl.ds(pl.multiple_of(i * c, c), c)
            p = p_s[rows, :]
            bm, xs = b_s[rows, :], x_s[rows, :]
            sf = st_s[:, :gw]
            if with_out:
                cm = c_s[rows, :]
                diff_mat = jnp.concatenate([dm_ref[0:l_one, :], nb_s[i], dm_ref[l_one + nd:, :]], axis=0)
                d = _dot(p, diff_mat)
                base = _dot_nt(cm, bm)
                exw = _dot(p, ex_ref[:, :2 * gw])
                o_s[rows, :] = exw[:, :gw] * _dot(cm, sf.astype(BF16))
                dtt = dtt_s[i]
                for r in range(hpg):
                    rb = hpg + r
                    hd = slice(r * M2_HEADDIM, (r + 1) * M2_HEADDIM)
                    mf = jnp.where(lower, jnp.exp(jnp.minimum(d[:, r * c:(r + 1) * c], 0.0)) * dtt[r:r + 1, :], 0.0)
                    mb = jnp.where(upper, jnp.exp(jnp.minimum(d[:, rb * c:(rb + 1) * c], 0.0)) * dtt[rb:rb + 1, :], 0.0)
                    m = (base * (mf + mb)).astype(BF16)
                    o_s[rows, hd] += _dot(m, x_s[rows, hd])
                wexp = exw[:, gw:]
            else:
                wexp = _dot(p, ex_ref[:, gw:2 * gw])
            xw = (xs.astype(F32) * wexp).astype(BF16)
            st_s[:, :gw] = de_s[i][:, :gw] * sf + _dot_tn(bm, xw)
            return carry

        lax.fori_loop(0, n, step, 0)

    def backward(x_s, b_s, c_s, z_ref, y_ref, o_s, n, with_out):
        def step(j, carry):
            i = n - 1 - j
            rows = pl.ds(pl.multiple_of(i * c, c), c)
            bm, xs = b_s[rows, :], x_s[rows, :]
            xsf = xs.astype(F32)
            sb = st_s[:, gw:]
            ex = _dot(p_s[rows, :], ex_ref[:, 2 * gw:])
            if with_out:
                y = o_s[rows, :] + ex[:, :gw] * _dot(c_s[rows, :], sb.astype(BF16)) + dskip_ref[...] * xsf
                y = y * _silu(z_ref[0, rows, :].astype(F32))
                y_ref[0, rows, :] = (_rms(y) * gn_ref[...]).astype(y_ref.dtype)
            xw = (xsf * ex[:, gw:]).astype(BF16)
            st_s[:, gw:] = de_s[i][:, gw:] * sb + _dot_tn(bm, xw)
            return carry

        lax.fori_loop(0, n, step, 0)

    conv_silu(xc_ref, xc_s, cwx_ref, cbx_ref, n_ctx)
    conv_silu(bc_ref, bc_s, cwb_ref, cbb_ref, n_ctx)
    conv_silu(cc_ref, cc_s, cwc_ref, cbc_ref, n_ctx)
    conv_silu(xl_ref, xl_s, cwx_ref, cbx_ref, n_lat)
    conv_silu(bl_ref, bl_s, cwb_ref, cbb_ref, n_lat)
    conv_silu(cl_ref, cl_s, cwc_ref, cbc_ref, n_lat)
    st_s[...] = jnp.zeros_like(st_s)
    prepare(dtc_ref, n_ctx)
    forward(xc_s, bc_s, cc_s, oc_s, n_ctx, need_ctx)
    backward(xc_s, bc_s, cc_s, zc_ref, yc_ref, oc_s, n_ctx, need_ctx)
    prepare(dtl_ref, n_lat)
    forward(xl_s, bl_s, cl_s, ol_s, n_lat, True)
    backward(xl_s, bl_s, cl_s, zl_ref, yl_ref, ol_s, n_lat, True)


def _m2_group_lanes(v, groups):
    lead = v.shape[:-1]
    hpg = v.shape[-1] // (2 * groups)
    v = v.reshape(lead + (2, groups, hpg))
    v = jnp.moveaxis(v, -3, -2).reshape(lead + (groups, 2 * hpg))
    pad = jnp.zeros(lead + (groups, LANES - 2 * hpg), v.dtype)
    return jnp.concatenate([v, pad], axis=-1).reshape(lead + (groups * LANES,))


def _mamba2_scan(p_c, dt_c, p_l, dt_l, conv_w, conv_b, dt_bias, a_neg, d_skip, gn, need_ctx):
    bt, length, _ = p_l.shape
    n_c = p_c.shape[1]
    dinner = gn.shape[0]
    heads = dinner // M2_HEADDIM
    g = M2_GROUPS
    hpg = heads // g
    gw = dinner // g
    ns = M2_DSTATE
    conv_dim = conv_w.shape[1]
    zb = dinner // gw
    xb = 2 * dinner // ns

    bias_sel = _m2_group_lanes(dt_bias.astype(F32).reshape(1, -1), g).reshape(g, 1, LANES)
    aneg_sel = _m2_group_lanes(a_neg.astype(F32).reshape(1, -1), g).reshape(g, 1, LANES)
    head_of_lane = jnp.arange(gw) // M2_HEADDIM

    def spread(first_lane):
        return (jnp.arange(LANES)[:, None] == first_lane + head_of_lane[None, :]).astype(BF16)

    nd = 2 * hpg
    l_mid, l_lo, l_e, l_w, l_one = nd, 2 * nd, 4 * nd, 5 * nd, 6 * nd
    ex = jnp.concatenate([spread(l_e), spread(l_w), spread(l_e + hpg), spread(l_w + hpg)], axis=1)
    ede = jnp.concatenate([spread(0), spread(hpg)], axis=1)
    col_head = jnp.arange(nd * M2_CHUNK) // M2_CHUNK
    rows_idx = jnp.arange(LANES)[:, None]
    diff_const = ((rows_idx < l_lo + nd) & (rows_idx % nd == col_head[None, :])).astype(BF16)
    d_exp = jnp.repeat(d_skip.astype(F32), M2_HEADDIM).reshape(1, dinner)

    def cols(rows, width, off):
        return pl.BlockSpec((1, rows, width), lambda b, j: (b, 0, off + j))

    def specs(rows):
        return [cols(rows, gw, 0), cols(rows, gw, zb), cols(rows, ns, xb), cols(rows, ns, xb + g),
                pl.BlockSpec((1, rows, LANES), lambda b, j: (b, 0, j))]

    def cvec(nrows, width, off):
        return pl.BlockSpec((nrows, width), lambda b, j: (0, off + j))

    per_group = pl.BlockSpec((1, 1, LANES), lambda b, j: (j, 0, 0))
    in_specs = (specs(n_c) + specs(length)
                + [cvec(3, gw, 0), cvec(3, ns, dinner // ns), cvec(3, ns, dinner // ns + g),
                   cvec(1, gw, 0), cvec(1, ns, dinner // ns), cvec(1, ns, dinner // ns + g),
                   per_group, per_group,
                   _resident(ex.shape), _resident(ede.shape), _resident(diff_const.shape),
                   cvec(1, gw, 0), cvec(1, gw, 0)])
    out_specs = [cols(length, gw, 0)]
    out_shape = [jax.ShapeDtypeStruct((bt, length, dinner), BF16)]
    if need_ctx:
        out_specs.insert(0, cols(n_c, gw, 0))
        out_shape.insert(0, jax.ShapeDtypeStruct((bt, n_c, dinner), BF16))
    scratch = [pltpu.VMEM((length, gw), BF16), pltpu.VMEM((length, ns), BF16), pltpu.VMEM((length, ns), BF16),
               pltpu.VMEM((n_c, gw), BF16), pltpu.VMEM((n_c, ns), BF16), pltpu.VMEM((n_c, ns), BF16),
               pltpu.VMEM((length, gw), F32), pltpu.VMEM((n_c, gw), F32),
               pltpu.VMEM((ns, 2 * gw), F32),
               pltpu.VMEM((length, LANES), BF16), pltpu.VMEM((length // M2_CHUNK, nd, M2_CHUNK), F32),
               pltpu.VMEM((length // M2_CHUNK, nd, nd * M2_CHUNK), BF16),
               pltpu.VMEM((length // M2_CHUNK, 1, 2 * gw), F32)]
    cb = conv_b.reshape(1, conv_dim)
    res = pl.pallas_call(
        functools.partial(_m2_kernel, need_ctx=need_ctx, chunk=M2_CHUNK),
        grid=(bt, g),
        in_specs=in_specs, out_specs=out_specs, out_shape=out_shape, scratch_shapes=scratch,
        compiler_params=_cparams("parallel", "parallel"),
        name="mamba2_scan",
    )(p_c, p_c, p_c, p_c, dt_c, p_l, p_l, p_l, p_l, dt_l,
      conv_w, conv_w, conv_w, cb, cb, cb, bias_sel, aneg_sel, ex, ede, diff_const, d_exp,
      gn.reshape(1, dinner))
    return (res[0], res[1]) if need_ctx else (None, res[0])


def kernel(x, c, ctx, c_ctx, ada_w, ada_b, norm_g, ret_w_in, ret_w_out, ret_decay, ret_gn, hg_w_in, hg_w_out, hg_lb, hg_gn, m2_w_in, m2_w_out, m2_conv_w, m2_conv_b, m2_dt_bias, m2_a_log, m2_d, m2_gn, ffn_w_up, ffn_conv_w, ffn_conv_b, ffn_w_down):
    bt, length, d = x.shape
    n_c = ctx.shape[1]
    depth = ada_w.shape[0]
    hidden = ffn_conv_w.shape[-1]
    tm = min(512, length)
    tm_c = n_c

    rows = -(-(bt + 1) // 8) * 8
    cc = jnp.concatenate([c, c_ctx[None, :], jnp.zeros((rows - bt - 1, d), F32)], axis=0)
    mod = _modulation(cc, ada_w, ada_b)

    lb_cum = jnp.cumsum(jax.nn.softmax(hg_lb.astype(F32), axis=0), axis=0)
    lb_all = lb_cum - lb_cum[0]

    h_l, h_c = x, ctx
    for i in range(depth):
        kind, j = i % N_MIXERS, i // N_MIXERS
        need_ctx = i < depth - 1
        mod_l = mod[i, :bt].reshape(bt, 6, 1, d)
        mod_c = mod[i, bt].reshape(1, 6, 1, d)
        sh1_l, sc1_l, g1_l, sh2_l, sc2_l, g2_l = (mod_l[:, k] for k in range(6))
        sh1_c, sc1_c, g1_c, sh2_c, sc2_c, g2_c = (mod_c[:, k] for k in range(6))

        if kind == 0:
            w_in = ret_w_in[j].astype(BF16)
            p_l = _project(h_l, norm_g[i, 0], sc1_l, sh1_l, w_in, tm=tm)
            p_c = _project(h_c, norm_g[i, 0], sc1_c, sh1_c, w_in, tm=tm_c)
            log_gamma = jax.nn.log_sigmoid(ret_decay[j].astype(F32))
            y_c, y_l = _retention_scan(p_c, p_l, log_gamma, ret_gn[j], need_ctx)
            w_out = ret_w_out[j].astype(BF16)
        elif kind == 1:
            wq, wff, wfb, wi, wg = jnp.split(hg_w_in[j], 5, axis=-1)
            w_main = jnp.concatenate([wq, wi, wg], axis=-1).astype(BF16)
            w_gate = jnp.concatenate([wff, wfb], axis=-1).astype(BF16)
            p_l, f_l = _project(h_l, norm_g[i, 0], sc1_l, sh1_l, w_main, w_gate, tm=tm)
            p_c, f_c = _project(h_c, norm_g[i, 0], sc1_c, sh1_c, w_main, w_gate, tm=tm_c)
            y_c, y_l = _hgrn2_scan(p_c, f_c, p_l, f_l, lb_all[i], hg_gn[j], need_ctx)
            w_out = hg_w_out[j].astype(BF16)
        else:
            dinner = m2_gn.shape[-1]
            n_main = dinner + m2_conv_w.shape[-1]
            w_main = m2_w_in[j][:, :n_main].astype(BF16)
            w_dt = _m2_group_lanes(m2_w_in[j][:, n_main:], M2_GROUPS).astype(BF16)
            p_l, dt_l = _project(h_l, norm_g[i, 0], sc1_l, sh1_l, w_main, w_dt, tm=tm)
            p_c, dt_c = _project(h_c, norm_g[i, 0], sc1_c, sh1_c, w_main, w_dt, tm=tm_c)
            a_neg = -jnp.exp(m2_a_log[j].astype(F32))
            y_c, y_l = _mamba2_scan(p_c, dt_c, p_l, dt_l, m2_conv_w[j], m2_conv_b[j], m2_dt_bias[j], a_neg,
                                    m2_d[j], m2_gn[j], need_ctx)
            w_out = m2_w_out[j].astype(BF16)

        wa = ffn_w_up[i][:, :hidden].astype(BF16)
        wv = ffn_w_up[i][:, hidden:].astype(BF16)
        wd = ffn_w_down[i].astype(BF16)
        h_l = _out_project(y_l, w_out, h_l, norm_g[i, 1], g1_l, tm=tm)
        h_l = _conv_ffn(h_l, norm_g[i, 2], sc2_l, sh2_l, wa, wv, ffn_conv_w[i], ffn_conv_b[i], wd,
                        norm_g[i, 3], g2_l, tm=tm, seg=GRID_W)
        if need_ctx:
            h_c = _out_project(y_c, w_out, h_c, norm_g[i, 1], g1_c, tm=tm_c)
            h_c = _conv_ffn(h_c, norm_g[i, 2], sc2_c, sh2_c, wa, wv, ffn_conv_w[i], ffn_conv_b[i], wd,
                            norm_g[i, 3], g2_c, tm=tm_c, seg=n_c)
    return h_l
```

```python
import functools
import math

import jax
import jax.numpy as jnp
from jax import lax
from jax.experimental import pallas as pl
from jax.experimental.pallas import tpu as pltpu

F32 = jnp.float32
BF16 = jnp.bfloat16

NORM_EPS = 1e-6
GRID_W = 64
ROPE_BASE = 10000.0
N_MIXERS = 3

RET_HEADS = 4
RET_CHUNK = 256
HG_DK = 128
HG_CHUNK = 64
HG_BLOCK = 512
HG_SAFE_LOG_DECAY = -80.0
M2_HEADDIM = 64
M2_GROUPS = 4
M2_DSTATE = 128
M2_CHUNK = 128
LANES = 128

V7X_VMEM_LIMIT_BYTES = 56 * 1024 * 1024

_HI = lax.Precision.HIGHEST


def _cparams(*sem):
    return pltpu.CompilerParams(dimension_semantics=sem, vmem_limit_bytes=V7X_VMEM_LIMIT_BYTES)


def _sigmoid(x):
    return 1.0 / (1.0 + jnp.exp(-x))


def _silu(x):
    return x * _sigmoid(x)


def _rms(t):
    return t * lax.rsqrt(jnp.mean(t * t, axis=-1, keepdims=True) + NORM_EPS)


def _dot(a, b):
    return jnp.dot(a, b, preferred_element_type=F32)


def _dot_nt(a, b):
    return lax.dot_general(a, b, (((1,), (1,)), ((), ())), preferred_element_type=F32)


def _dot_tn(a, b):
    return lax.dot_general(a, b, (((0,), (0,)), ((), ())), preferred_element_type=F32)


def _aligned_rows(start, size):
    return pl.ds(start if isinstance(start, int) else pl.multiple_of(start, size), size)


def _resident(shape):
    return pl.BlockSpec(shape, lambda *_: (0,) * len(shape), pipeline_mode=pl.Buffered(1))


def _mod_kernel(c_ref, w_ref, b_ref, o_ref):
    s = _silu(c_ref[...]).astype(BF16)
    o_ref[0] = _dot(s, w_ref[0].astype(BF16)) + b_ref[0]


def _modulation(cc, ada_w, ada_b):
    depth, d, n = ada_w.shape
    rows = cc.shape[0]
    tn = 1536
    return pl.pallas_call(
        _mod_kernel,
        grid=(depth, n // tn),
        in_specs=[pl.BlockSpec((rows, d), lambda i, j: (0, 0)),
                  pl.BlockSpec((1, d, tn), lambda i, j: (i, 0, j)),
                  pl.BlockSpec((1, 1, tn), lambda i, j: (i, 0, j))],
        out_specs=pl.BlockSpec((1, rows, tn), lambda i, j: (i, 0, j)),
        out_shape=jax.ShapeDtypeStruct((depth, rows, n), F32),
        compiler_params=_cparams("parallel", "parallel"),
        name="adaln_mod",
    )(cc, ada_w, ada_b.reshape(depth, 1, n))


def _norm_mod(x, g, sc, sh):
    return _rms(x) * (g * (1.0 + sc)) + sh


def _proj_kernel(x_ref, g_ref, sc_ref, sh_ref, w_ref, *rest, ncol):
    u = _norm_mod(x_ref[0], g_ref[...], sc_ref[0], sh_ref[0]).astype(BF16)
    if len(rest) == 3:
        w2_ref, o_ref, o2_ref = rest
        o2_ref[0] = _dot(u, w2_ref[...])
    else:
        (o_ref,) = rest
    n = w_ref.shape[1]
    for n0 in range(0, n, ncol):
        n1 = min(n0 + ncol, n)
        o_ref[0, :, n0:n1] = _dot(u, w_ref[:, n0:n1]).astype(o_ref.dtype)


def _project(h, g, sc, sh, w, w2=None, *, tm):
    bt, length, d = h.shape
    n = w.shape[1]
    per_batch = sc.shape[0] != 1
    mod_spec = pl.BlockSpec((1, 1, d), (lambda b, i: (b, 0, 0)) if per_batch else (lambda b, i: (0, 0, 0)))
    in_specs = [pl.BlockSpec((1, tm, d), lambda b, i: (b, i, 0)),
                pl.BlockSpec((1, d), lambda b, i: (0, 0)),
                mod_spec, mod_spec, _resident((d, n))]
    out_specs = [pl.BlockSpec((1, tm, n), lambda b, i: (b, i, 0))]
    out_shape = [jax.ShapeDtypeStruct((bt, length, n), BF16)]
    args = [h, g.reshape(1, d), sc, sh, w]
    if w2 is not None:
        n2 = w2.shape[1]
        in_specs.append(_resident((d, n2)))
        out_specs.append(pl.BlockSpec((1, tm, n2), lambda b, i: (b, i, 0)))
        out_shape.append(jax.ShapeDtypeStruct((bt, length, n2), F32))
        args.append(w2)
    res = pl.pallas_call(
        functools.partial(_proj_kernel, ncol=512),
        grid=(bt, length // tm),
        in_specs=in_specs, out_specs=out_specs, out_shape=out_shape,
        compiler_params=_cparams("parallel", "parallel"),
        name="norm_mod_proj",
    )(*args)
    return res if w2 is not None else res[0]


def _out_kernel(y_ref, w_ref, h_ref, ng_ref, gate_ref, o_ref):
    t = _dot(y_ref[0], w_ref[...])
    o_ref[0] = h_ref[0] + gate_ref[0] * (_rms(t) * ng_ref[...])


def _out_project(y, w, h, ng, gate, *, tm):
    bt, length, d = h.shape
    dy = y.shape[-1]
    per_batch = gate.shape[0] != 1
    gate_spec = pl.BlockSpec((1, 1, d), (lambda b, i: (b, 0, 0)) if per_batch else (lambda b, i: (0, 0, 0)))
    return pl.pallas_call(
        _out_kernel,
        grid=(bt, length // tm),
        in_specs=[pl.BlockSpec((1, tm, dy), lambda b, i: (b, i, 0)),
                  _resident((dy, d)),
                  pl.BlockSpec((1, tm, d), lambda b, i: (b, i, 0)),
                  pl.BlockSpec((1, d), lambda b, i: (0, 0)),
                  gate_spec],
        out_specs=pl.BlockSpec((1, tm, d), lambda b, i: (b, i, 0)),
        out_shape=jax.ShapeDtypeStruct((bt, length, d), F32),
        compiler_params=_cparams("parallel", "parallel"),
        name="out_proj_residual",
    )(y, w, h, ng.reshape(1, d), gate)


def _gelu_tanh(x):
    return 0.5 * x * (1.0 + jnp.tanh(math.sqrt(2.0 / math.pi) * (x + 0.044715 * (x * x * x))))


def _ffn_kernel(h_ref, g_ref, sc_ref, sh_ref, wa_ref, wv_ref, cw_ref, cb_ref, wd_ref, ng_ref, gate_ref, o_ref,
                *, seg, hidden_chunks):
    h = h_ref[0]
    tm = h.shape[0]
    u = _norm_mod(h, g_ref[...], sc_ref[0], sh_ref[0]).astype(BF16)
    pos = lax.broadcasted_iota(jnp.int32, (tm, 1), 0) % seg
    has_prev = pos != 0
    has_next = pos != seg - 1
    acc = None
    for c0, c1 in hidden_chunks:
        a = _dot(u, wa_ref[:, c0:c1])
        v = _dot(u, wv_ref[:, c0:c1])
        a_prev = jnp.where(has_prev, pltpu.roll(a, 1, 0), 0.0)
        a_next = jnp.where(has_next, pltpu.roll(a, tm - 1, 0), 0.0)
        ac = (cw_ref[0:1, c0:c1] * a_prev + cw_ref[1:2, c0:c1] * a + cw_ref[2:3, c0:c1] * a_next
              + cb_ref[:, c0:c1])
        hid = (_gelu_tanh(ac) * v).astype(BF16)
        part = _dot(hid, wd_ref[c0:c1, :])
        acc = part if acc is None else acc + part
    o_ref[0] = h + gate_ref[0] * (_rms(acc) * ng_ref[...])


def _hidden_chunks(hidden, size):
    return tuple((c0, min(c0 + size, hidden)) for c0 in range(0, hidden, size))


def _conv_ffn(h, g, sc, sh, wa, wv, cw, cb, wd, ng, gate, *, tm, seg):
    bt, length, d = h.shape
    hidden = wa.shape[1]
    per_batch = sc.shape[0] != 1
    mod_spec = pl.BlockSpec((1, 1, d), (lambda b, i: (b, 0, 0)) if per_batch else (lambda b, i: (0, 0, 0)))
    vec_d = pl.BlockSpec((1, d), lambda b, i: (0, 0))
    return pl.pallas_call(
        functools.partial(_ffn_kernel, seg=seg, hidden_chunks=_hidden_chunks(hidden, 1024)),
        grid=(bt, length // tm),
        in_specs=[pl.BlockSpec((1, tm, d), lambda b, i: (b, i, 0)),
                  vec_d, mod_spec, mod_spec,
                  _resident((d, hidden)), _resident((d, hidden)),
                  pl.BlockSpec((3, hidden), lambda b, i: (0, 0)),
                  pl.BlockSpec((1, hidden), lambda b, i: (0, 0)),
                  _resident((hidden, d)),
                  vec_d, mod_spec],
        out_specs=pl.BlockSpec((1, tm, d), lambda b, i: (b, i, 0)),
        out_shape=jax.ShapeDtypeStruct((bt, length, d), F32),
        compiler_params=_cparams("parallel", "parallel"),
        name="conv_glu_ffn",
    )(h, g.reshape(1, d), sc, sh, wa, wv, cw, cb.reshape(1, hidden), wd, ng.reshape(1, d), gate)


def _ret_kernel(lg_ref, qc_ref, kc_ref, vc_ref, gc_ref, ql_ref, kl_ref, vl_ref, gl_ref, gn_ref,
                rcos_ref, rsin_ref, ccos_ref, csin_ref, *rest, need_ctx, chunk):
    if need_ctx:
        yc_ref, yl_ref, qr_s, kr_s, qcs_s, kcs_s, ol_s, oc_s, sf_s, sb_s = rest
    else:
        yl_ref, qr_s, kr_s, qcs_s, kcs_s, ol_s, oc_s, sf_s, sb_s = rest
        yc_ref = None
    c = chunk
    head = pl.program_id(1)
    lgf = lg_ref[0, head]
    lgb = lg_ref[1, head]
    dk = ql_ref.shape[-1]
    k_scale = dk ** -0.5
    n_lat = ql_ref.shape[1] // c
    n_ctx = qc_ref.shape[1] // c

    ti = lax.broadcasted_iota(jnp.int32, (c, 1), 0).astype(F32)
    dq_f = jnp.exp(lgf * (ti + 1.0))
    dk_f = jnp.exp(lgf * (c - 1.0 - ti))
    dq_b = jnp.exp(lgb * (c - ti))
    dk_b = jnp.exp(lgb * ti)
    full_chunk = jnp.full((1, 1), float(c), F32)
    dc_f = jnp.exp(lgf * full_chunk)
    dc_b = jnp.exp(lgb * full_chunk)
    diff = (lax.broadcasted_iota(jnp.int32, (c, c), 0) - lax.broadcasted_iota(jnp.int32, (c, c), 1)).astype(F32)
    dmask = (jnp.where(diff >= 0, jnp.exp(lgf * jnp.maximum(diff, 0.0)), 0.0)
             + jnp.where(diff <= 0, jnp.exp(lgb * jnp.maximum(-diff, 0.0)), 0.0))

    half = dk // 2

    def rope_row(r, carry):
        rows = pl.ds(pl.multiple_of(r * GRID_W, GRID_W), GRID_W)
        cr = rcos_ref[pl.ds(r, 1), :]
        sr = rsin_ref[pl.ds(r, 1), :]
        for src, dst, scale in ((ql_ref, qr_s, 1.0), (kl_ref, kr_s, k_scale)):
            t = src[0, rows, :].astype(F32) * scale
            t0, t1 = t[:, :half], t[:, half:]
            dst[rows, :half] = (t0 * cr + pltpu.roll(t0, half // 2, 1) * sr).astype(BF16)
            dst[rows, half:] = (t1 * ccos_ref[...] + pltpu.roll(t1, half // 2, 1) * csin_ref[...]).astype(BF16)
        return carry

    lax.fori_loop(0, ql_ref.shape[1] // GRID_W, rope_row, 0, unroll=4)
    qcs_s[...] = qc_ref[0]
    kcs_s[...] = (kc_ref[0].astype(F32) * k_scale).astype(BF16)

    sf_s[...] = jnp.zeros_like(sf_s)
    sb_s[...] = jnp.zeros_like(sb_s)

    def scan(q_s, k_s, v_ref, g_ref, y_ref, o_s, n):
        with_out = y_ref is not None

        def chunk(i):
            return pl.ds(i * c if isinstance(i, int) else pl.multiple_of(i * c, c), c)

        def forward(rows):
            qf, kf, vf = q_s[rows, :], k_s[rows, :], v_ref[0, rows, :]
            o = None
            if with_out:
                p = (_dot_nt(qf, kf) * dmask).astype(BF16)
                o = _dot(p, vf) + dq_f * _dot(qf, sf_s[...].astype(BF16))
            kd = (kf.astype(F32) * dk_f).astype(BF16)
            sf_s[...] = dc_f * sf_s[...] + _dot_tn(kd, vf)
            return o

        def backward(rows):
            qb, kb, vb = q_s[rows, :], k_s[rows, :], v_ref[0, rows, :]
            o = dq_b * _dot(qb, sb_s[...].astype(BF16)) if with_out else None
            kdb = (kb.astype(F32) * dk_b).astype(BF16)
            sb_s[...] = dc_b * sb_s[...] + _dot_tn(kdb, vb)
            return o

        def finish(rows, o):
            o = o - jnp.mean(o, axis=-1, keepdims=True)
            o = _rms(o) * gn_ref[...]
            y_ref[0, rows, :] = (_silu(g_ref[0, rows, :].astype(F32)) * o).astype(y_ref.dtype)

        def first_half(i, carry):
            fw, bw = chunk(i), chunk(n - 1 - i)
            of, ob = forward(fw), backward(bw)
            if with_out:
                o_s[fw, :] = of
                o_s[bw, :] = ob
            return carry

        def second_half(i, carry):
            fw, bw = chunk(i), chunk(n - 1 - i)
            of, ob = forward(fw), backward(bw)
            if with_out:
                finish(fw, o_s[fw, :] + of)
                finish(bw, o_s[bw, :] + ob)
            return carry

        lax.fori_loop(0, n // 2, first_half, 0)
        if n % 2:
            mid = chunk(n // 2)
            of, ob = forward(mid), backward(mid)
            if with_out:
                finish(mid, of + ob)
        lax.fori_loop(n - n // 2, n, second_half, 0)

    scan(qcs_s, kcs_s, vc_ref, gc_ref, yc_ref, oc_s, n_ctx)
    scan(qr_s, kr_s, vl_ref, gl_ref, yl_ref, ol_s, n_lat)


def _rope_tables(length, dk):
    nf = dk // 4
    inv = ROPE_BASE ** (-jnp.arange(nf, dtype=F32) / nf)

    def tab(p):
        ang = p[:, None] * inv
        cos, sin = jnp.cos(ang), jnp.sin(ang)
        return jnp.concatenate([cos, cos], -1), jnp.concatenate([-sin, sin], -1)

    rcos, rsin = tab(jnp.arange(length // GRID_W).astype(F32))
    ccos, csin = tab(jnp.arange(GRID_W).astype(F32))
    return rcos, rsin, ccos, csin


def _retention_scan(p_c, p_l, log_gamma, gn, need_ctx):
    bt, length, _ = p_l.shape
    n_c = p_c.shape[1]
    hv = gn.shape[0]
    dv = hv // RET_HEADS
    dk = dv // 2
    h = RET_HEADS
    tables = _rope_tables(length, dk)

    def cols(rows, width, off):
        return pl.BlockSpec((1, rows, width), lambda b, j: (b, 0, off + j))

    def specs(rows):
        return [cols(rows, dk, 0), cols(rows, dk, h), cols(rows, dv, h), cols(rows, dv, 2 * h)]

    in_specs = ([pl.BlockSpec(memory_space=pltpu.SMEM)] + specs(n_c) + specs(length)
                + [pl.BlockSpec((1, dv), lambda b, j: (0, j))]
                + [_resident(t.shape) for t in tables])
    out_specs = [cols(length, dv, 0)]
    out_shape = [jax.ShapeDtypeStruct((bt, length, hv), BF16)]
    if need_ctx:
        out_specs.insert(0, cols(n_c, dv, 0))
        out_shape.insert(0, jax.ShapeDtypeStruct((bt, n_c, hv), BF16))
    scratch = [pltpu.VMEM((length, dk), BF16), pltpu.VMEM((length, dk), BF16),
               pltpu.VMEM((n_c, dk), BF16), pltpu.VMEM((n_c, dk), BF16),
               pltpu.VMEM((length, dv), F32), pltpu.VMEM((n_c, dv), F32),
               pltpu.VMEM((dk, dv), F32), pltpu.VMEM((dk, dv), F32)]
    res = pl.pallas_call(
        functools.partial(_ret_kernel, need_ctx=need_ctx, chunk=RET_CHUNK),
        grid=(bt, h),
        in_specs=in_specs, out_specs=out_specs, out_shape=out_shape, scratch_shapes=scratch,
        compiler_params=_cparams("parallel", "parallel"),
        name="retention_scan",
    )(log_gamma, p_c, p_c, p_c, p_c, p_l, p_l, p_l, p_l, gn.reshape(1, hv), *tables)
    return (res[0], res[1]) if need_ctx else (None, res[0])


def _bf16_terms(x):
    hi = x.astype(BF16)
    r1 = x - hi.astype(F32)
    mid = r1.astype(BF16)
    lo = (r1 - mid.astype(F32)).astype(BF16)
    return hi, mid, lo


def _hg_kernel(qc_ref, ic_ref, gc_ref, fc_ref, ql_ref, il_ref, gl_ref, fl_ref, lb_ref, gn_ref, *rest,
               need_ctx, chunk, nblk):
    if need_ctx:
        yc_ref, yl_ref = rest[:2]
        rest = rest[2:]
    else:
        yc_ref, yl_ref = None, rest[0]
        rest = rest[1:]
    ol_s, oc_s, st_f, st_b, sv_s, qd_s, kd_s, kl_s, eb_s, mn_s, tb_s, tk_s, tv_s = rest
    c = chunk
    step = pl.program_id(1)
    d = lb_ref.shape[-1]
    heads = d // HG_DK
    t = ql_ref.shape[1]
    n_lat = t // c
    n_ctx = qc_ref.shape[1] // c
    row = lax.broadcasted_iota(jnp.int32, (c, c), 0)
    col = lax.broadcasted_iota(jnp.int32, (c, c), 1)
    lower = row >= col
    upper = row <= col
    tidx = lax.broadcasted_iota(jnp.int32, (c, 1), 0)

    def gates(f, lb):
        e = jnp.exp(-jnp.abs(f))
        r = 1.0 / (1.0 + e)
        pos = f >= 0
        sig = jnp.where(pos, r, e * r)
        sig_neg = jnp.where(pos, e * r, r)
        return jnp.log(lb + (1.0 - lb) * sig), (1.0 - lb) * sig_neg

    def cumsum(mask, la):
        tri = mask.astype(BF16)
        hi, mid, lo = _bf16_terms(la)
        return _dot(tri, hi) + (_dot(tri, mid) + _dot(tri, lo))

    def sweep(q_ref, i_ref, f_ref, fcols, n, mask, last, st, reverse, emit):
        lb = lb_ref[...]

        def chunk_rows(j):
            i = (n - 1 - j) if reverse else j
            return i, _aligned_rows(i * c, c)

        def prepare(j, slot):
            _, rows = chunk_rows(j)
            la, kk = gates(f_ref[0, rows, fcols], lb)
            b = cumsum(mask, la)
            b_end = b[last:last + 1, :]
            qd_s[slot] = (_silu(q_ref[0, rows, :].astype(F32)) * jnp.exp(b)).astype(BF16)
            kd_s[slot] = (kk * jnp.exp(-b)).astype(BF16)
            kl_s[slot] = (kk * jnp.exp(b_end - b)).astype(BF16)
            eb_s[slot] = jnp.exp(b_end)
            mn_s[...] = jnp.minimum(mn_s[...], b_end)

        def fast(j, slot):
            _, rows = chunk_rows(j)
            for h in range(heads):
                hc = slice(h * HG_DK, (h + 1) * HG_DK)
                kl, v = kl_s[slot, :, hc], i_ref[0, rows, hc]
                if emit is not None:
                    qd = qd_s[slot, :, hc]
                    s = jnp.where(mask, _dot_nt(qd, kd_s[slot, :, hc]), 0.0).astype(BF16)
                    emit(rows, hc, _dot(s, v) + _dot_nt(qd, st[h].astype(BF16)))
                st[h] = eb_s[slot, :, hc] * st[h] + _dot_tn(v, kl)

        sv_s[...] = st[...]
        mn_s[...] = jnp.zeros_like(mn_s)
        prepare(0, 0)

        def body(jj, carry):
            j = 2 * jj
            prepare(j + 1, 1)
            fast(j, 0)
            prepare(j + 2, 0)
            fast(j + 1, 1)
            return carry

        lax.fori_loop(0, n // 2 - 1, body, 0)
        prepare(n - 1, 1)
        fast(n - 2, 0)
        fast(n - 1, 1)

        @pl.when(jnp.min(mn_s[...]) < HG_SAFE_LOG_DECAY)
        def _exact():
            st[...] = sv_s[...]

            def body(j, carry):
                i, rows = chunk_rows(j)
                for h in range(heads):
                    hc = slice(h * HG_DK, (h + 1) * HG_DK)
                    fh = f_ref[0, rows, hc] if fcols == slice(None) else f_ref[0, rows, fcols.start + h * HG_DK:
                                                                               fcols.start + (h + 1) * HG_DK]
                    la, kk = gates(fh, lb_ref[:, hc])
                    b = cumsum(mask, la)
                    b_end = b[last:last + 1, :]
                    v = i_ref[0, rows, hc]
                    if emit is not None:
                        qs = _silu(q_ref[0, rows, hc].astype(F32))
                        tb_s[...] = b
                        tk_s[...] = kk
                        tv_s[...] = v.astype(F32)

                        def inner(s, acc):
                            one = pl.ds(s, 1)
                            dec = jnp.exp(jnp.minimum(b - tb_s[one, :], 0.0))
                            w = jnp.sum(qs * tk_s[one, :] * dec, axis=-1, keepdims=True)
                            keep = (tidx <= s) if reverse else (tidx >= s)
                            return acc + jnp.where(keep, w, 0.0) * tv_s[one, :]

                        o = lax.fori_loop(0, c, inner, jnp.zeros((c, HG_DK), F32))
                        qd = (qs * jnp.exp(b)).astype(BF16)
                        emit(rows, hc, o + _dot_nt(qd, st[h].astype(BF16)))
                    kl = (kk * jnp.exp(b_end - b)).astype(BF16)
                    st[h] = jnp.exp(b_end) * st[h] + _dot_tn(v, kl)
                return carry

            lax.fori_loop(0, n, body, 0)

    def store_partial(o_s, row0):
        def emit(rows, hc, o):
            o_s[_aligned_rows(row0 + rows.start, c), hc] = o
        return emit

    def finish(o_s, row0, g_ref, y_ref):
        def emit(rows, hc, o):
            total = o_s[_aligned_rows(row0 + rows.start, c), hc] + o
            y = _silu(g_ref[0, rows, hc].astype(F32)) * (_rms(total) * gn_ref[:, hc])
            y_ref[0, rows, hc] = y.astype(y_ref.dtype)
        return emit

    @pl.when(step == 0)
    def _context():
        st_f[...] = jnp.zeros_like(st_f)
        st_b[...] = jnp.zeros_like(st_b)
        sweep(qc_ref, ic_ref, fc_ref, slice(0, d), n_ctx, lower, c - 1, st_f, False,
              store_partial(oc_s, 0) if need_ctx else None)
        sweep(qc_ref, ic_ref, fc_ref, slice(d, 2 * d), n_ctx, upper, 0, st_b, True,
              finish(oc_s, 0, gc_ref, yc_ref) if need_ctx else None)

    @pl.when(step < nblk)
    def _forward():
        sweep(ql_ref, il_ref, fl_ref, slice(None), n_lat, lower, c - 1, st_f, False,
              store_partial(ol_s, step * t))

    @pl.when(step >= nblk)
    def _backward():
        sweep(ql_ref, il_ref, fl_ref, slice(None), n_lat, upper, 0, st_b, True,
              finish(ol_s, (2 * nblk - 1 - step) * t, gl_ref, yl_ref))


def _hgrn2_scan(p_c, f_c, p_l, f_l, lb, gn, need_ctx):
    bt, length, d3 = p_l.shape
    d = d3 // 3
    n_c = p_c.shape[1]
    h = d // HG_DK
    t = min(HG_BLOCK, length)
    nblk = length // t
    c = HG_CHUNK

    def blk(s):
        return jnp.where(s < nblk, s, 2 * nblk - 1 - s)

    def bwd_blk(s):
        return jnp.where(s < nblk, nblk - 1, 2 * nblk - 1 - s)

    lat_specs = [pl.BlockSpec((1, t, d), lambda b, s: (b, blk(s), 0)),
                 pl.BlockSpec((1, t, d), lambda b, s: (b, blk(s), 1)),
                 pl.BlockSpec((1, t, d), lambda b, s: (b, bwd_blk(s), 2)),
                 pl.BlockSpec((1, t, d), lambda b, s: (b, blk(s), jnp.where(s < nblk, 0, 1)))]
    ctx_specs = [pl.BlockSpec((1, n_c, d), lambda b, s: (b, 0, 0)),
                 pl.BlockSpec((1, n_c, d), lambda b, s: (b, 0, 1)),
                 pl.BlockSpec((1, n_c, d), lambda b, s: (b, 0, 2)),
                 pl.BlockSpec((1, n_c, 2 * d), lambda b, s: (b, 0, 0))]
    vec = pl.BlockSpec((1, d), lambda b, s: (0, 0))
    out_specs = [pl.BlockSpec((1, t, d), lambda b, s: (b, bwd_blk(s), 0))]
    out_shape = [jax.ShapeDtypeStruct((bt, length, d), BF16)]
    if need_ctx:
        out_specs.insert(0, pl.BlockSpec((1, n_c, d), lambda b, s: (b, 0, 0)))
        out_shape.insert(0, jax.ShapeDtypeStruct((bt, n_c, d), BF16))
    rows = max(t, n_c)
    scratch = [pltpu.VMEM((length, d), F32), pltpu.VMEM((n_c, d), F32),
               pltpu.VMEM((h, HG_DK, HG_DK), F32), pltpu.VMEM((h, HG_DK, HG_DK), F32),
               pltpu.VMEM((h, HG_DK, HG_DK), F32),
               pltpu.VMEM((2, c, d), BF16), pltpu.VMEM((2, c, d), BF16), pltpu.VMEM((2, c, d), BF16),
               pltpu.VMEM((2, 1, d), F32), pltpu.VMEM((1, d), F32),
               pltpu.VMEM((c, HG_DK), F32), pltpu.VMEM((c, HG_DK), F32), pltpu.VMEM((c, HG_DK), F32)]
    res = pl.pallas_call(
        functools.partial(_hg_kernel, need_ctx=need_ctx, chunk=c, nblk=nblk),
        grid=(bt, 2 * nblk),
        in_specs=ctx_specs + lat_specs + [vec, vec],
        out_specs=out_specs, out_shape=out_shape, scratch_shapes=scratch,
        compiler_params=_cparams("parallel", "arbitrary"),
        name="hgrn2_scan",
    )(p_c, p_c, p_c, f_c, p_l, p_l, p_l, f_l, lb.reshape(1, d), gn.reshape(1, d))
    return (res[0], res[1]) if need_ctx else (None, res[0])


def _dot_terms(x, m):
    hi, mid, lo = _bf16_terms(x)
    return _dot(hi, m) + (_dot(mid, m) + _dot(lo, m))


def _m2_kernel(zc_ref, xc_ref, bc_ref, cc_ref, dtc_ref, zl_ref, xl_ref, bl_ref, cl_ref, dtl_ref,
               cwx_ref, cwb_ref, cwc_ref, cbx_ref, cbb_ref, cbc_ref,
               bias_ref, aneg_ref, ex_ref, ede_ref, dm_ref, dskip_ref, gn_ref, *rest, need_ctx, chunk):
    if need_ctx:
        yc_ref, yl_ref = rest[:2]
        rest = rest[2:]
    else:
        yc_ref, yl_ref = None, rest[0]
        rest = rest[1:]
    xl_s, bl_s, cl_s, xc_s, bc_s, cc_s, ol_s, oc_s, st_s, p_s, de_s, pc_s, dtt_s, nb_s, dec_s = rest
    c = chunk
    n_lat = xl_ref.shape[1] // c
    n_ctx = xc_ref.shape[1] // c
    hpg = xl_ref.shape[-1] // M2_HEADDIM
    gw = hpg * M2_HEADDIM
    row = lax.broadcasted_iota(jnp.int32, (c, c), 0)
    col = lax.broadcasted_iota(jnp.int32, (c, c), 1)
    lower = row >= col
    upper = row <= col
    tril = lower.astype(BF16)
    triu = upper.astype(BF16)
    ridx = lax.broadcasted_iota(jnp.int32, (c, 1), 0)
    lane = lax.broadcasted_iota(jnp.int32, (1, LANES), 1)
    fwd_lane = lane < hpg
    nd = 2 * hpg
    l_mid, l_lo, l_dt, l_e, l_w, l_one = (k * nd for k in range(1, 7))

    def conv_silu(src_ref, dst_s, w_ref, b_ref, n):
        total = n * c
        halo = 16

        def step(i, carry):
            start = pl.multiple_of(i * c, c)
            rows = pl.ds(start, c)
            x = src_ref[0, rows, :].astype(F32)
            prev_blk = src_ref[0, pl.ds(pl.multiple_of(jnp.maximum(start - halo, 0), halo), halo), :]
            next_blk = src_ref[0, pl.ds(pl.multiple_of(jnp.minimum(start + c, total - halo), halo), halo), :]
            prev_row = jnp.where(i > 0, prev_blk[halo - 1:halo, :].astype(F32), 0.0)
            next_row = jnp.where(i < n - 1, next_blk[0:1, :].astype(F32), 0.0)
            x_prev = jnp.where(ridx == 0, prev_row, pltpu.roll(x, 1, 0))
            x_next = jnp.where(ridx == c - 1, next_row, pltpu.roll(x, c - 1, 0))
            y = w_ref[0:1, :] * x_prev + w_ref[1:2, :] * x + w_ref[2:3, :] * x_next + b_ref[...]
            dst_s[rows, :] = _silu(y).astype(dst_s.dtype)
            return carry

        lax.fori_loop(0, n, step, 0)

    def prepare(dt_ref):
        def step(i, slot):
            rows = _aligned_rows(i * c, c)
            x = dt_ref[0, rows, :] + bias_ref[0]
            dt = jnp.maximum(x, 0.0) + jnp.log(1.0 + jnp.exp(-jnp.abs(x)))
            hi, mid, lo = _bf16_terms(dt * aneg_ref[0])
            bf = _dot(tril, hi) + (_dot(tril, mid) + _dot(tril, lo))
            bb = _dot(triu, hi) + (_dot(triu, mid) + _dot(triu, lo))
            b = jnp.where(fwd_lane, bf, bb)
            b_end = jnp.where(fwd_lane, bf[c - 1:c, :], bb[0:1, :])
            w = jnp.exp(b_end - b) * dt
            hi, mid, lo = (t.astype(F32) for t in _bf16_terms(b))
            q = jnp.where(lane < l_one + 3, 1.0, 0.0)
            for first, field in ((l_w, w), (l_e, jnp.exp(b)), (l_dt, dt), (l_lo, lo), (l_mid, mid)):
                q = jnp.where(lane < first + nd, pltpu.roll(field, first, 1), q)
            q = jnp.where(lane < nd, hi, q)
            p_s[rows, :] = q.astype(BF16)
            pc_s[slot] = q.astype(BF16)
            qt = q.T
            dtt_s[slot] = qt[l_dt:l_dt + nd, :]
            neg = [jnp.concatenate([-qt[first + r:first + r + 1, :] for r in range(nd)], axis=1)
                   for first in (0, l_mid, l_lo)]
            nb_s[slot] = jnp.concatenate(neg + [jnp.zeros((nd - 3, nd * c), F32)], axis=0).astype(BF16)
            de = _dot_terms(jnp.broadcast_to(jnp.exp(b_end), (8, LANES)), ede_ref[...])[0:1, :]
            de_s[i] = de
            dec_s[slot] = de

        return step

    def forward(dt_ref, x_s, b_s, c_s, o_s, n, with_out):
        prepare_chunk = prepare(dt_ref)

        def step(i, slot):
            rows = _aligned_rows(i * c, c)
            p = pc_s[slot]
            bm, xs = b_s[rows, :], x_s[rows, :]
            sf = st_s[:, :gw]
            if with_out:
                cm = c_s[rows, :]
                diff_mat = jnp.concatenate([dm_ref[0:l_one, :], nb_s[slot], dm_ref[l_one + nd:, :]], axis=0)
                d = _dot(p, diff_mat)
                base = _dot_nt(cm, bm)
                exw = _dot(p, ex_ref[:, :2 * gw])
                o_s[rows, :] = exw[:, :gw] * _dot(cm, sf.astype(BF16))
                dtt = dtt_s[slot]
                for r in range(hpg):
                    rb = hpg + r
                    hd = slice(r * M2_HEADDIM, (r + 1) * M2_HEADDIM)
                    mf = jnp.where(lower, jnp.exp(jnp.minimum(d[:, r * c:(r + 1) * c], 0.0)) * dtt[r:r + 1, :], 0.0)
                    mb = jnp.where(upper, jnp.exp(jnp.minimum(d[:, rb * c:(rb + 1) * c], 0.0)) * dtt[rb:rb + 1, :], 0.0)
                    m = (base * (mf + mb)).astype(BF16)
                    o_s[rows, hd] += _dot(m, x_s[rows, hd])
                wexp = exw[:, gw:]
            else:
                wexp = _dot(p, ex_ref[:, gw:2 * gw])
            xw = (xs.astype(F32) * wexp).astype(BF16)
            st_s[:, :gw] = dec_s[slot][:, :gw] * sf + _dot_tn(bm, xw)

        def body(ii, carry):
            i = 2 * ii
            prepare_chunk(i + 1, 1)
            step(i, 0)
            prepare_chunk(i + 2, 0)
            step(i + 1, 1)
            return carry

        prepare_chunk(0, 0)
        lax.fori_loop(0, n // 2 - 1, body, 0)
        prepare_chunk(n - 1, 1)
        step(n - 2, 0)
        step(n - 1, 1)

    def backward(x_s, b_s, c_s, z_ref, y_ref, o_s, n, with_out):
        def step(j, carry):
            i = n - 1 - j
            rows = pl.ds(pl.multiple_of(i * c, c), c)
            bm, xs = b_s[rows, :], x_s[rows, :]
            xsf = xs.astype(F32)
            sb = st_s[:, gw:]
            ex = _dot(p_s[rows, :], ex_ref[:, 2 * gw:])
            if with_out:
                y = o_s[rows, :] + ex[:, :gw] * _dot(c_s[rows, :], sb.astype(BF16)) + dskip_ref[...] * xsf
                y = y * _silu(z_ref[0, rows, :].astype(F32))
                y_ref[0, rows, :] = (_rms(y) * gn_ref[...]).astype(y_ref.dtype)
            xw = (xsf * ex[:, gw:]).astype(BF16)
            st_s[:, gw:] = de_s[i][:, gw:] * sb + _dot_tn(bm, xw)
            return carry

        lax.fori_loop(0, n, step, 0)

    conv_silu(xc_ref, xc_s, cwx_ref, cbx_ref, n_ctx)
    conv_silu(bc_ref, bc_s, cwb_ref, cbb_ref, n_ctx)
    conv_silu(cc_ref, cc_s, cwc_ref, cbc_ref, n_ctx)
    conv_silu(xl_ref, xl_s, cwx_ref, cbx_ref, n_lat)
    conv_silu(bl_ref, bl_s, cwb_ref, cbb_ref, n_lat)
    conv_silu(cl_ref, cl_s, cwc_ref, cbc_ref, n_lat)
    st_s[...] = jnp.zeros_like(st_s)
    forward(dtc_ref, xc_s, bc_s, cc_s, oc_s, n_ctx, need_ctx)
    backward(xc_s, bc_s, cc_s, zc_ref, yc_ref, oc_s, n_ctx, need_ctx)
    forward(dtl_ref, xl_s, bl_s, cl_s, ol_s, n_lat, True)
    backward(xl_s, bl_s, cl_s, zl_ref, yl_ref, ol_s, n_lat, True)


def _m2_group_lanes(v, groups):
    lead = v.shape[:-1]
    hpg = v.shape[-1] // (2 * groups)
    v = v.reshape(lead + (2, groups, hpg))
    v = jnp.moveaxis(v, -3, -2).reshape(lead + (groups, 2 * hpg))
    pad = jnp.zeros(lead + (groups, LANES - 2 * hpg), v.dtype)
    return jnp.concatenate([v, pad], axis=-1).reshape(lead + (groups * LANES,))


def _mamba2_scan(p_c, dt_c, p_l, dt_l, conv_w, conv_b, dt_bias, a_neg, d_skip, gn, need_ctx):
    bt, length, _ = p_l.shape
    n_c = p_c.shape[1]
    dinner = gn.shape[0]
    heads = dinner // M2_HEADDIM
    g = M2_GROUPS
    hpg = heads // g
    gw = dinner // g
    ns = M2_DSTATE
    conv_dim = conv_w.shape[1]
    zb = dinner // gw
    xb = 2 * dinner // ns

    bias_sel = _m2_group_lanes(dt_bias.astype(F32).reshape(1, -1), g).reshape(g, 1, LANES)
    aneg_sel = _m2_group_lanes(a_neg.astype(F32).reshape(1, -1), g).reshape(g, 1, LANES)
    head_of_lane = jnp.arange(gw) // M2_HEADDIM

    def spread(first_lane):
        return (jnp.arange(LANES)[:, None] == first_lane + head_of_lane[None, :]).astype(BF16)

    nd = 2 * hpg
    l_mid, l_lo, l_e, l_w, l_one = nd, 2 * nd, 4 * nd, 5 * nd, 6 * nd
    ex = jnp.concatenate([spread(l_e), spread(l_w), spread(l_e + hpg), spread(l_w + hpg)], axis=1)
    ede = jnp.concatenate([spread(0), spread(hpg)], axis=1)
    col_head = jnp.arange(nd * M2_CHUNK) // M2_CHUNK
    rows_idx = jnp.arange(LANES)[:, None]
    diff_const = ((rows_idx < l_lo + nd) & (rows_idx % nd == col_head[None, :])).astype(BF16)
    d_exp = jnp.repeat(d_skip.astype(F32), M2_HEADDIM).reshape(1, dinner)

    def cols(rows, width, off):
        return pl.BlockSpec((1, rows, width), lambda b, j: (b, 0, off + j))

    def specs(rows):
        return [cols(rows, gw, 0), cols(rows, gw, zb), cols(rows, ns, xb), cols(rows, ns, xb + g),
                pl.BlockSpec((1, rows, LANES), lambda b, j: (b, 0, j), pipeline_mode=pl.Buffered(1))]

    def cvec(nrows, width, off):
        return pl.BlockSpec((nrows, width), lambda b, j: (0, off + j))

    per_group = pl.BlockSpec((1, 1, LANES), lambda b, j: (j, 0, 0))
    in_specs = (specs(n_c) + specs(length)
                + [cvec(3, gw, 0), cvec(3, ns, dinner // ns), cvec(3, ns, dinner // ns + g),
                   cvec(1, gw, 0), cvec(1, ns, dinner // ns), cvec(1, ns, dinner // ns + g),
                   per_group, per_group,
                   _resident(ex.shape), _resident(ede.shape), _resident(diff_const.shape),
                   cvec(1, gw, 0), cvec(1, gw, 0)])
    out_specs = [cols(length, gw, 0)]
    out_shape = [jax.ShapeDtypeStruct((bt, length, dinner), BF16)]
    if need_ctx:
        out_specs.insert(0, cols(n_c, gw, 0))
        out_shape.insert(0, jax.ShapeDtypeStruct((bt, n_c, dinner), BF16))
    scratch = [pltpu.VMEM((length, gw), BF16), pltpu.VMEM((length, ns), BF16), pltpu.VMEM((length, ns), BF16),
               pltpu.VMEM((n_c, gw), BF16), pltpu.VMEM((n_c, ns), BF16), pltpu.VMEM((n_c, ns), BF16),
               pltpu.VMEM((length, gw), F32), pltpu.VMEM((n_c, gw), F32),
               pltpu.VMEM((ns, 2 * gw), F32),
               pltpu.VMEM((length, LANES), BF16), pltpu.VMEM((length // M2_CHUNK, 1, 2 * gw), F32),
               pltpu.VMEM((2, M2_CHUNK, LANES), BF16), pltpu.VMEM((2, nd, M2_CHUNK), F32),
               pltpu.VMEM((2, nd, nd * M2_CHUNK), BF16), pltpu.VMEM((2, 1, 2 * gw), F32)]
    cb = conv_b.reshape(1, conv_dim)
    res = pl.pallas_call(
        functools.partial(_m2_kernel, need_ctx=need_ctx, chunk=M2_CHUNK),
        grid=(bt, g),
        in_specs=in_specs, out_specs=out_specs, out_shape=out_shape, scratch_shapes=scratch,
        compiler_params=_cparams("parallel", "parallel"),
        name="mamba2_scan",
    )(p_c, p_c, p_c, p_c, dt_c, p_l, p_l, p_l, p_l, dt_l,
      conv_w, conv_w, conv_w, cb, cb, cb, bias_sel, aneg_sel, ex, ede, diff_const, d_exp,
      gn.reshape(1, dinner))
    return (res[0], res[1]) if need_ctx else (None, res[0])


def kernel(x, c, ctx, c_ctx, ada_w, ada_b, norm_g, ret_w_in, ret_w_out, ret_decay, ret_gn, hg_w_in, hg_w_out, hg_lb, hg_gn, m2_w_in, m2_w_out, m2_conv_w, m2_conv_b, m2_dt_bias, m2_a_log, m2_d, m2_gn, ffn_w_up, ffn_conv_w, ffn_conv_b, ffn_w_down):
    bt, length, d = x.shape
    n_c = ctx.shape[1]
    depth = ada_w.shape[0]
    hidden = ffn_conv_w.shape[-1]
    tm = min(512, length)
    tm_c = n_c

    rows = -(-(bt + 1) // 8) * 8
    cc = jnp.concatenate([c, c_ctx[None, :], jnp.zeros((rows - bt - 1, d), F32)], axis=0)
    mod = _modulation(cc, ada_w, ada_b)

    lb_cum = jnp.cumsum(jax.nn.softmax(hg_lb.astype(F32), axis=0), axis=0)
    lb_all = lb_cum - lb_cum[0]

    h_l, h_c = x, ctx
    for i in range(depth):
        kind, j = i % N_MIXERS, i // N_MIXERS
        need_ctx = i < depth - 1
        mod_l = mod[i, :bt].reshape(bt, 6, 1, d)
        mod_c = mod[i, bt].reshape(1, 6, 1, d)
        sh1_l, sc1_l, g1_l, sh2_l, sc2_l, g2_l = (mod_l[:, k] for k in range(6))
        sh1_c, sc1_c, g1_c, sh2_c, sc2_c, g2_c = (mod_c[:, k] for k in range(6))

        if kind == 0:
            w_in = ret_w_in[j].astype(BF16)
            p_l = _project(h_l, norm_g[i, 0], sc1_l, sh1_l, w_in, tm=tm)
            p_c = _project(h_c, norm_g[i, 0], sc1_c, sh1_c, w_in, tm=tm_c)
            log_gamma = jax.nn.log_sigmoid(ret_decay[j].astype(F32))
            y_c, y_l = _retention_scan(p_c, p_l, log_gamma, ret_gn[j], need_ctx)
            w_out = ret_w_out[j].astype(BF16)
        elif kind == 1:
            wq, wff, wfb, wi, wg = jnp.split(hg_w_in[j], 5, axis=-1)
            w_main = jnp.concatenate([wq, wi, wg], axis=-1).astype(BF16)
            w_gate = jnp.concatenate([wff, wfb], axis=-1).astype(BF16)
            p_l, f_l = _project(h_l, norm_g[i, 0], sc1_l, sh1_l, w_main, w_gate, tm=tm)
            p_c, f_c = _project(h_c, norm_g[i, 0], sc1_c, sh1_c, w_main, w_gate, tm=tm_c)
            y_c, y_l = _hgrn2_scan(p_c, f_c, p_l, f_l, lb_all[i], hg_gn[j], need_ctx)
            w_out = hg_w_out[j].astype(BF16)
        else:
            dinner = m2_gn.shape[-1]
            n_main = dinner + m2_conv_w.shape[-1]
            w_main = m2_w_in[j][:, :n_main].astype(BF16)
            w_dt = _m2_group_lanes(m2_w_in[j][:, n_main:], M2_GROUPS).astype(BF16)
            p_l, dt_l = _project(h_l, norm_g[i, 0], sc1_l, sh1_l, w_main, w_dt, tm=tm)
            p_c, dt_c = _project(h_c, norm_g[i, 0], sc1_c, sh1_c, w_main, w_dt, tm=tm_c)
            a_neg = -jnp.exp(m2_a_log[j].astype(F32))
            y_c, y_l = _mamba2_scan(p_c, dt_c, p_l, dt_l, m2_conv_w[j], m2_conv_b[j], m2_dt_bias[j], a_neg,
                                    m2_d[j], m2_gn[j], need_ctx)
            w_out = m2_w_out[j].astype(BF16)

        wa = ffn_w_up[i][:, :hidden].astype(BF16)
        wv = ffn_w_up[i][:, hidden:].astype(BF16)
        wd = ffn_w_down[i].astype(BF16)
        h_l = _out_project(y_l, w_out, h_l, norm_g[i, 1], g1_l, tm=tm)
        h_l = _conv_ffn(h_l, norm_g[i, 2], sc2_l, sh2_l, wa, wv, ffn_conv_w[i], ffn_conv_b[i], wd,
                        norm_g[i, 3], g2_l, tm=tm, seg=GRID_W)
        if need_ctx:
            h_c = _out_project(y_c, w_out, h_c, norm_g[i, 1], g1_c, tm=tm_c)
            h_c = _conv_ffn(h_c, norm_g[i, 2], sc2_c, sh2_c, wa, wv, ffn_conv_w[i], ffn_conv_b[i], wd,
                            norm_g[i, 3], g2_c, tm=tm_c, seg=n_c)
    return h_l
```

```python
import functools
import math

import jax
import jax.numpy as jnp
from jax import lax
from jax.experimental import pallas as pl
from jax.experimental.pallas import tpu as pltpu

F32 = jnp.float32
BF16 = jnp.bfloat16

NORM_EPS = 1e-6
GRID_W = 64
ROPE_BASE = 10000.0
N_MIXERS = 3

RET_HEADS = 4
RET_CHUNK = 256
HG_DK = 128
HG_CHUNK = 64
HG_BLOCK = 1024
HG_SAFE_LOG_DECAY = -80.0
M2_HEADDIM = 64
M2_GROUPS = 4
M2_DSTATE = 128
M2_CHUNK = 128
LANES = 128

V7X_VMEM_LIMIT_BYTES = 56 * 1024 * 1024

_HI = lax.Precision.HIGHEST


def _cparams(*sem):
    return pltpu.CompilerParams(dimension_semantics=sem, vmem_limit_bytes=V7X_VMEM_LIMIT_BYTES)


def _sigmoid(x):
    return 1.0 / (1.0 + jnp.exp(-x))


def _silu(x):
    return x * _sigmoid(x)


def _rms(t):
    return t * lax.rsqrt(jnp.mean(t * t, axis=-1, keepdims=True) + NORM_EPS)


def _dot(a, b):
    return jnp.dot(a, b, preferred_element_type=F32)


def _dot_nt(a, b):
    return lax.dot_general(a, b, (((1,), (1,)), ((), ())), preferred_element_type=F32)


def _dot_tn(a, b):
    return lax.dot_general(a, b, (((0,), (0,)), ((), ())), preferred_element_type=F32)


def _aligned_rows(start, size):
    return pl.ds(start if isinstance(start, int) else pl.multiple_of(start, size), size)


def _resident(shape):
    return pl.BlockSpec(shape, lambda *_: (0,) * len(shape), pipeline_mode=pl.Buffered(1))


def _mod_kernel(c_ref, w_ref, b_ref, o_ref):
    s = _silu(c_ref[...]).astype(BF16)
    o_ref[0] = _dot(s, w_ref[0].astype(BF16)) + b_ref[0]


def _modulation(cc, ada_w, ada_b):
    depth, d, n = ada_w.shape
    rows = cc.shape[0]
    tn = 1536
    return pl.pallas_call(
        _mod_kernel,
        grid=(depth, n // tn),
        in_specs=[pl.BlockSpec((rows, d), lambda i, j: (0, 0)),
                  pl.BlockSpec((1, d, tn), lambda i, j: (i, 0, j)),
                  pl.BlockSpec((1, 1, tn), lambda i, j: (i, 0, j))],
        out_specs=pl.BlockSpec((1, rows, tn), lambda i, j: (i, 0, j)),
        out_shape=jax.ShapeDtypeStruct((depth, rows, n), F32),
        compiler_params=_cparams("parallel", "parallel"),
        name="adaln_mod",
    )(cc, ada_w, ada_b.reshape(depth, 1, n))


def _norm_mod(x, g, sc, sh):
    return _rms(x) * (g * (1.0 + sc)) + sh


def _proj_kernel(x_ref, g_ref, sc_ref, sh_ref, w_ref, *rest, ncol):
    u = _norm_mod(x_ref[0], g_ref[...], sc_ref[0], sh_ref[0]).astype(BF16)
    if len(rest) == 3:
        w2_ref, o_ref, o2_ref = rest
        o2_ref[0] = _dot(u, w2_ref[...])
    else:
        (o_ref,) = rest
    n = w_ref.shape[1]
    for n0 in range(0, n, ncol):
        n1 = min(n0 + ncol, n)
        o_ref[0, :, n0:n1] = _dot(u, w_ref[:, n0:n1]).astype(o_ref.dtype)


def _project(h, g, sc, sh, w, w2=None, *, tm):
    bt, length, d = h.shape
    n = w.shape[1]
    per_batch = sc.shape[0] != 1
    mod_spec = pl.BlockSpec((1, 1, d), (lambda b, i: (b, 0, 0)) if per_batch else (lambda b, i: (0, 0, 0)))
    in_specs = [pl.BlockSpec((1, tm, d), lambda b, i: (b, i, 0)),
                pl.BlockSpec((1, d), lambda b, i: (0, 0)),
                mod_spec, mod_spec, _resident((d, n))]
    out_specs = [pl.BlockSpec((1, tm, n), lambda b, i: (b, i, 0))]
    out_shape = [jax.ShapeDtypeStruct((bt, length, n), BF16)]
    args = [h, g.reshape(1, d), sc, sh, w]
    if w2 is not None:
        n2 = w2.shape[1]
        in_specs.append(_resident((d, n2)))
        out_specs.append(pl.BlockSpec((1, tm, n2), lambda b, i: (b, i, 0)))
        out_shape.append(jax.ShapeDtypeStruct((bt, length, n2), F32))
        args.append(w2)
    res = pl.pallas_call(
        functools.partial(_proj_kernel, ncol=512),
        grid=(bt, length // tm),
        in_specs=in_specs, out_specs=out_specs, out_shape=out_shape,
        compiler_params=_cparams("parallel", "parallel"),
        name="norm_mod_proj",
    )(*args)
    return res if w2 is not None else res[0]


def _out_kernel(y_ref, w_ref, h_ref, ng_ref, gate_ref, o_ref):
    t = _dot(y_ref[0], w_ref[...])
    o_ref[0] = h_ref[0] + gate_ref[0] * (_rms(t) * ng_ref[...])


def _out_project(y, w, h, ng, gate, *, tm):
    bt, length, d = h.shape
    dy = y.shape[-1]
    per_batch = gate.shape[0] != 1
    gate_spec = pl.BlockSpec((1, 1, d), (lambda b, i: (b, 0, 0)) if per_batch else (lambda b, i: (0, 0, 0)))
    return pl.pallas_call(
        _out_kernel,
        grid=(bt, length // tm),
        in_specs=[pl.BlockSpec((1, tm, dy), lambda b, i: (b, i, 0)),
                  _resident((dy, d)),
                  pl.BlockSpec((1, tm, d), lambda b, i: (b, i, 0)),
                  pl.BlockSpec((1, d), lambda b, i: (0, 0)),
                  gate_spec],
        out_specs=pl.BlockSpec((1, tm, d), lambda b, i: (b, i, 0)),
        out_shape=jax.ShapeDtypeStruct((bt, length, d), F32),
        compiler_params=_cparams("parallel", "parallel"),
        name="out_proj_residual",
    )(y, w, h, ng.reshape(1, d), gate)


def _gelu_tanh(x):
    return 0.5 * x * (1.0 + jnp.tanh(math.sqrt(2.0 / math.pi) * (x + 0.044715 * (x * x * x))))


def _ffn_kernel(h_ref, g_ref, sc_ref, sh_ref, wa_ref, wv_ref, cw_ref, cb_ref, wd_ref, ng_ref, gate_ref, o_ref,
                *, seg, hidden_chunks):
    h = h_ref[0]
    tm = h.shape[0]
    u = _norm_mod(h, g_ref[...], sc_ref[0], sh_ref[0]).astype(BF16)
    pos = lax.broadcasted_iota(jnp.int32, (tm, 1), 0) % seg
    has_prev = pos != 0
    has_next = pos != seg - 1
    acc = None
    for c0, c1 in hidden_chunks:
        a = _dot(u, wa_ref[:, c0:c1])
        v = _dot(u, wv_ref[:, c0:c1])
        a_prev = jnp.where(has_prev, pltpu.roll(a, 1, 0), 0.0)
        a_next = jnp.where(has_next, pltpu.roll(a, tm - 1, 0), 0.0)
        ac = (cw_ref[0:1, c0:c1] * a_prev + cw_ref[1:2, c0:c1] * a + cw_ref[2:3, c0:c1] * a_next
              + cb_ref[:, c0:c1])
        hid = (_gelu_tanh(ac) * v).astype(BF16)
        part = _dot(hid, wd_ref[c0:c1, :])
        acc = part if acc is None else acc + part
    o_ref[0] = h + gate_ref[0] * (_rms(acc) * ng_ref[...])


def _hidden_chunks(hidden, size):
    return tuple((c0, min(c0 + size, hidden)) for c0 in range(0, hidden, size))


def _conv_ffn(h, g, sc, sh, wa, wv, cw, cb, wd, ng, gate, *, tm, seg):
    bt, length, d = h.shape
    hidden = wa.shape[1]
    per_batch = sc.shape[0] != 1
    mod_spec = pl.BlockSpec((1, 1, d), (lambda b, i: (b, 0, 0)) if per_batch else (lambda b, i: (0, 0, 0)))
    vec_d = pl.BlockSpec((1, d), lambda b, i: (0, 0))
    return pl.pallas_call(
        functools.partial(_ffn_kernel, seg=seg, hidden_chunks=_hidden_chunks(hidden, 2816)),
        grid=(bt, length // tm),
        in_specs=[pl.BlockSpec((1, tm, d), lambda b, i: (b, i, 0)),
                  vec_d, mod_spec, mod_spec,
                  _resident((d, hidden)), _resident((d, hidden)),
                  pl.BlockSpec((3, hidden), lambda b, i: (0, 0)),
                  pl.BlockSpec((1, hidden), lambda b, i: (0, 0)),
                  _resident((hidden, d)),
                  vec_d, mod_spec],
        out_specs=pl.BlockSpec((1, tm, d), lambda b, i: (b, i, 0)),
        out_shape=jax.ShapeDtypeStruct((bt, length, d), F32),
        compiler_params=_cparams("parallel", "parallel"),
        name="conv_glu_ffn",
    )(h, g.reshape(1, d), sc, sh, wa, wv, cw, cb.reshape(1, hidden), wd, ng.reshape(1, d), gate)


def _ret_kernel(lg_ref, qc_ref, kc_ref, vc_ref, gc_ref, ql_ref, kl_ref, vl_ref, gl_ref, gn_ref,
                rcos_ref, rsin_ref, ccos_ref, csin_ref, *rest, need_ctx, chunk):
    if need_ctx:
        yc_ref, yl_ref, qr_s, kr_s, qcs_s, kcs_s, ol_s, oc_s, sf_s, sb_s = rest
    else:
        yl_ref, qr_s, kr_s, qcs_s, kcs_s, ol_s, oc_s, sf_s, sb_s = rest
        yc_ref = None
    c = chunk
    head = pl.program_id(1)
    lgf = lg_ref[0, head]
    lgb = lg_ref[1, head]
    dk = ql_ref.shape[-1]
    k_scale = dk ** -0.5
    n_lat = ql_ref.shape[1] // c
    n_ctx = qc_ref.shape[1] // c

    ti = lax.broadcasted_iota(jnp.int32, (c, 1), 0).astype(F32)
    dq_f = jnp.exp(lgf * (ti + 1.0))
    dk_f = jnp.exp(lgf * (c - 1.0 - ti))
    dq_b = jnp.exp(lgb * (c - ti))
    dk_b = jnp.exp(lgb * ti)
    full_chunk = jnp.full((1, 1), float(c), F32)
    dc_f = jnp.exp(lgf * full_chunk)
    dc_b = jnp.exp(lgb * full_chunk)
    diff = (lax.broadcasted_iota(jnp.int32, (c, c), 0) - lax.broadcasted_iota(jnp.int32, (c, c), 1)).astype(F32)
    dmask = (jnp.where(diff >= 0, jnp.exp(lgf * jnp.maximum(diff, 0.0)), 0.0)
             + jnp.where(diff <= 0, jnp.exp(lgb * jnp.maximum(-diff, 0.0)), 0.0))

    half = dk // 2

    def rope_row(r, carry):
        rows = pl.ds(pl.multiple_of(r * GRID_W, GRID_W), GRID_W)
        cr = rcos_ref[pl.ds(r, 1), :]
        sr = rsin_ref[pl.ds(r, 1), :]
        for src, dst, scale in ((ql_ref, qr_s, 1.0), (kl_ref, kr_s, k_scale)):
            t = src[0, rows, :].astype(F32) * scale
            t0, t1 = t[:, :half], t[:, half:]
            dst[rows, :half] = (t0 * cr + pltpu.roll(t0, half // 2, 1) * sr).astype(BF16)
            dst[rows, half:] = (t1 * ccos_ref[...] + pltpu.roll(t1, half // 2, 1) * csin_ref[...]).astype(BF16)
        return carry

    lax.fori_loop(0, ql_ref.shape[1] // GRID_W, rope_row, 0, unroll=4)
    qcs_s[...] = qc_ref[0]
    kcs_s[...] = (kc_ref[0].astype(F32) * k_scale).astype(BF16)

    sf_s[...] = jnp.zeros_like(sf_s)
    sb_s[...] = jnp.zeros_like(sb_s)

    def scan(q_s, k_s, v_ref, g_ref, y_ref, o_s, n):
        with_out = y_ref is not None

        def chunk(i):
            return pl.ds(i * c if isinstance(i, int) else pl.multiple_of(i * c, c), c)

        def forward(rows):
            qf, kf, vf = q_s[rows, :], k_s[rows, :], v_ref[0, rows, :]
            o = None
            if with_out:
                p = (_dot_nt(qf, kf) * dmask).astype(BF16)
                o = _dot(p, vf) + dq_f * _dot(qf, sf_s[...].astype(BF16))
            kd = (kf.astype(F32) * dk_f).astype(BF16)
            sf_s[...] = dc_f * sf_s[...] + _dot_tn(kd, vf)
            return o

        def backward(rows):
            qb, kb, vb = q_s[rows, :], k_s[rows, :], v_ref[0, rows, :]
            o = dq_b * _dot(qb, sb_s[...].astype(BF16)) if with_out else None
            kdb = (kb.astype(F32) * dk_b).astype(BF16)
            sb_s[...] = dc_b * sb_s[...] + _dot_tn(kdb, vb)
            return o

        def finish(rows, o):
            o = o - jnp.mean(o, axis=-1, keepdims=True)
            o = _rms(o) * gn_ref[...]
            y_ref[0, rows, :] = (_silu(g_ref[0, rows, :].astype(F32)) * o).astype(y_ref.dtype)

        def first_half(i, carry):
            fw, bw = chunk(i), chunk(n - 1 - i)
            of, ob = forward(fw), backward(bw)
            if with_out:
                o_s[fw, :] = of
                o_s[bw, :] = ob
            return carry

        def second_half(i, carry):
            fw, bw = chunk(i), chunk(n - 1 - i)
            of, ob = forward(fw), backward(bw)
            if with_out:
                finish(fw, o_s[fw, :] + of)
                finish(bw, o_s[bw, :] + ob)
            return carry

        lax.fori_loop(0, n // 2, first_half, 0)
        if n % 2:
            mid = chunk(n // 2)
            of, ob = forward(mid), backward(mid)
            if with_out:
                finish(mid, of + ob)
        lax.fori_loop(n - n // 2, n, second_half, 0)

    scan(qcs_s, kcs_s, vc_ref, gc_ref, yc_ref, oc_s, n_ctx)
    scan(qr_s, kr_s, vl_ref, gl_ref, yl_ref, ol_s, n_lat)


def _rope_tables(length, dk):
    nf = dk // 4
    inv = ROPE_BASE ** (-jnp.arange(nf, dtype=F32) / nf)

    def tab(p):
        ang = p[:, None] * inv
        cos, sin = jnp.cos(ang), jnp.sin(ang)
        return jnp.concatenate([cos, cos], -1), jnp.concatenate([-sin, sin], -1)

    rcos, rsin = tab(jnp.arange(length // GRID_W).astype(F32))
    ccos, csin = tab(jnp.arange(GRID_W).astype(F32))
    return rcos, rsin, ccos, csin


def _retention_scan(p_c, p_l, log_gamma, gn, need_ctx):
    bt, length, _ = p_l.shape
    n_c = p_c.shape[1]
    hv = gn.shape[0]
    dv = hv // RET_HEADS
    dk = dv // 2
    h = RET_HEADS
    tables = _rope_tables(length, dk)

    def cols(rows, width, off):
        return pl.BlockSpec((1, rows, width), lambda b, j: (b, 0, off + j))

    def specs(rows):
        return [cols(rows, dk, 0), cols(rows, dk, h), cols(rows, dv, h), cols(rows, dv, 2 * h)]

    in_specs = ([pl.BlockSpec(memory_space=pltpu.SMEM)] + specs(n_c) + specs(length)
                + [pl.BlockSpec((1, dv), lambda b, j: (0, j))]
                + [_resident(t.shape) for t in tables])
    out_specs = [cols(length, dv, 0)]
    out_shape = [jax.ShapeDtypeStruct((bt, length, hv), BF16)]
    if need_ctx:
        out_specs.insert(0, cols(n_c, dv, 0))
        out_shape.insert(0, jax.ShapeDtypeStruct((bt, n_c, hv), BF16))
    scratch = [pltpu.VMEM((length, dk), BF16), pltpu.VMEM((length, dk), BF16),
               pltpu.VMEM((n_c, dk), BF16), pltpu.VMEM((n_c, dk), BF16),
               pltpu.VMEM((length, dv), F32), pltpu.VMEM((n_c, dv), F32),
               pltpu.VMEM((dk, dv), F32), pltpu.VMEM((dk, dv), F32)]
    res = pl.pallas_call(
        functools.partial(_ret_kernel, need_ctx=need_ctx, chunk=RET_CHUNK),
        grid=(bt, h),
        in_specs=in_specs, out_specs=out_specs, out_shape=out_shape, scratch_shapes=scratch,
        compiler_params=_cparams("parallel", "parallel"),
        name="retention_scan",
    )(log_gamma, p_c, p_c, p_c, p_c, p_l, p_l, p_l, p_l, gn.reshape(1, hv), *tables)
    return (res[0], res[1]) if need_ctx else (None, res[0])


def _bf16_terms(x):
    hi = x.astype(BF16)
    r1 = x - hi.astype(F32)
    mid = r1.astype(BF16)
    lo = (r1 - mid.astype(F32)).astype(BF16)
    return hi, mid, lo


def _hg_kernel(qc_ref, ic_ref, gc_ref, fc_ref, ql_ref, il_ref, gl_ref, fl_ref, lb_ref, gn_ref, *rest,
               need_ctx, chunk, nblk):
    if need_ctx:
        yc_ref, yl_ref = rest[:2]
        rest = rest[2:]
    else:
        yc_ref, yl_ref = None, rest[0]
        rest = rest[1:]
    ol_s, oc_s, st_f, st_b, sv_s, qd_s, kd_s, kl_s, eb_s, mn_s, tb_s, tk_s, tv_s = rest
    c = chunk
    step = pl.program_id(1)
    d = lb_ref.shape[-1]
    heads = d // HG_DK
    t = ql_ref.shape[1]
    n_lat = t // c
    n_ctx = qc_ref.shape[1] // c
    row = lax.broadcasted_iota(jnp.int32, (c, c), 0)
    col = lax.broadcasted_iota(jnp.int32, (c, c), 1)
    lower = row >= col
    upper = row <= col
    tidx = lax.broadcasted_iota(jnp.int32, (c, 1), 0)

    def gates(f, lb):
        e = jnp.exp(-jnp.abs(f))
        r = 1.0 / (1.0 + e)
        pos = f >= 0
        sig = jnp.where(pos, r, e * r)
        sig_neg = jnp.where(pos, e * r, r)
        return jnp.log(lb + (1.0 - lb) * sig), (1.0 - lb) * sig_neg

    def cumsum(mask, la):
        tri = mask.astype(BF16)
        hi, mid, lo = _bf16_terms(la)
        return _dot(tri, hi) + (_dot(tri, mid) + _dot(tri, lo))

    def sweep(q_ref, i_ref, f_ref, fcols, n, mask, last, st, reverse, emit):
        lb = lb_ref[...]

        def chunk_rows(j):
            i = (n - 1 - j) if reverse else j
            return i, _aligned_rows(i * c, c)

        def prepare(j, slot):
            _, rows = chunk_rows(j)
            la, kk = gates(f_ref[0, rows, fcols], lb)
            b = cumsum(mask, la)
            b_end = b[last:last + 1, :]
            qd_s[slot] = (_silu(q_ref[0, rows, :].astype(F32)) * jnp.exp(b)).astype(BF16)
            kd = kk * jnp.exp(-b)
            e_end = jnp.exp(b_end)
            kd_s[slot] = kd.astype(BF16)
            kl_s[slot] = (kd * e_end).astype(BF16)
            eb_s[slot] = e_end
            mn_s[...] = jnp.minimum(mn_s[...], b_end)

        def fast(j, slot):
            _, rows = chunk_rows(j)
            for h in range(heads):
                hc = slice(h * HG_DK, (h + 1) * HG_DK)
                kl, v = kl_s[slot, :, hc], i_ref[0, rows, hc]
                if emit is not None:
                    qd = qd_s[slot, :, hc]
                    s = jnp.where(mask, _dot_nt(qd, kd_s[slot, :, hc]), 0.0).astype(BF16)
                    emit(rows, hc, _dot(s, v) + _dot_nt(qd, st[h].astype(BF16)))
                st[h] = eb_s[slot, :, hc] * st[h] + _dot_tn(v, kl)

        sv_s[...] = st[...]
        mn_s[...] = jnp.zeros_like(mn_s)
        prepare(0, 0)

        def body(jj, carry):
            j = 2 * jj
            prepare(j + 1, 1)
            fast(j, 0)
            prepare(j + 2, 0)
            fast(j + 1, 1)
            return carry

        lax.fori_loop(0, n // 2 - 1, body, 0)
        prepare(n - 1, 1)
        fast(n - 2, 0)
        fast(n - 1, 1)

        @pl.when(jnp.min(mn_s[...]) < HG_SAFE_LOG_DECAY)
        def _exact():
            st[...] = sv_s[...]

            def body(j, carry):
                i, rows = chunk_rows(j)
                for h in range(heads):
                    hc = slice(h * HG_DK, (h + 1) * HG_DK)
                    fh = f_ref[0, rows, hc] if fcols == slice(None) else f_ref[0, rows, fcols.start + h * HG_DK:
                                                                               fcols.start + (h + 1) * HG_DK]
                    la, kk = gates(fh, lb_ref[:, hc])
                    b = cumsum(mask, la)
                    b_end = b[last:last + 1, :]
                    v = i_ref[0, rows, hc]
                    if emit is not None:
                        qs = _silu(q_ref[0, rows, hc].astype(F32))
                        tb_s[...] = b
                        tk_s[...] = kk
                        tv_s[...] = v.astype(F32)

                        def inner(s, acc):
                            one = pl.ds(s, 1)
                            dec = jnp.exp(jnp.minimum(b - tb_s[one, :], 0.0))
                            w = jnp.sum(qs * tk_s[one, :] * dec, axis=-1, keepdims=True)
                            keep = (tidx <= s) if reverse else (tidx >= s)
                            return acc + jnp.where(keep, w, 0.0) * tv_s[one, :]

                        o = lax.fori_loop(0, c, inner, jnp.zeros((c, HG_DK), F32))
                        qd = (qs * jnp.exp(b)).astype(BF16)
                        emit(rows, hc, o + _dot_nt(qd, st[h].astype(BF16)))
                    kl = (kk * jnp.exp(b_end - b)).astype(BF16)
                    st[h] = jnp.exp(b_end) * st[h] + _dot_tn(v, kl)
                return carry

            lax.fori_loop(0, n, body, 0)

    def store_partial(o_s, row0):
        def emit(rows, hc, o):
            o_s[_aligned_rows(row0 + rows.start, c), hc] = o
        return emit

    def finish(o_s, row0, g_ref, y_ref):
        def emit(rows, hc, o):
            total = o_s[_aligned_rows(row0 + rows.start, c), hc] + o
            y = _silu(g_ref[0, rows, hc].astype(F32)) * (_rms(total) * gn_ref[:, hc])
            y_ref[0, rows, hc] = y.astype(y_ref.dtype)
        return emit

    @pl.when(step == 0)
    def _context():
        st_f[...] = jnp.zeros_like(st_f)
        st_b[...] = jnp.zeros_like(st_b)
        sweep(qc_ref, ic_ref, fc_ref, slice(0, d), n_ctx, lower, c - 1, st_f, False,
              store_partial(oc_s, 0) if need_ctx else None)
        sweep(qc_ref, ic_ref, fc_ref, slice(d, 2 * d), n_ctx, upper, 0, st_b, True,
              finish(oc_s, 0, gc_ref, yc_ref) if need_ctx else None)

    @pl.when(step < nblk)
    def _forward():
        sweep(ql_ref, il_ref, fl_ref, slice(None), n_lat, lower, c - 1, st_f, False,
              store_partial(ol_s, step * t))

    @pl.when(step >= nblk)
    def _backward():
        sweep(ql_ref, il_ref, fl_ref, slice(None), n_lat, upper, 0, st_b, True,
              finish(ol_s, (2 * nblk - 1 - step) * t, gl_ref, yl_ref))


def _hgrn2_scan(p_c, f_c, p_l, f_l, lb, gn, need_ctx):
    bt, length, d3 = p_l.shape
    d = d3 // 3
    n_c = p_c.shape[1]
    h = d // HG_DK
    t = min(HG_BLOCK, length)
    nblk = length // t
    c = HG_CHUNK

    def blk(s):
        return jnp.where(s < nblk, s, 2 * nblk - 1 - s)

    def bwd_blk(s):
        return jnp.where(s < nblk, nblk - 1, 2 * nblk - 1 - s)

    lat_specs = [pl.BlockSpec((1, t, d), lambda b, s: (b, blk(s), 0)),
                 pl.BlockSpec((1, t, d), lambda b, s: (b, blk(s), 1)),
                 pl.BlockSpec((1, t, d), lambda b, s: (b, bwd_blk(s), 2)),
                 pl.BlockSpec((1, t, d), lambda b, s: (b, blk(s), jnp.where(s < nblk, 0, 1)))]
    ctx_specs = [pl.BlockSpec((1, n_c, d), lambda b, s: (b, 0, 0)),
                 pl.BlockSpec((1, n_c, d), lambda b, s: (b, 0, 1)),
                 pl.BlockSpec((1, n_c, d), lambda b, s: (b, 0, 2)),
                 pl.BlockSpec((1, n_c, 2 * d), lambda b, s: (b, 0, 0))]
    vec = pl.BlockSpec((1, d), lambda b, s: (0, 0))
    out_specs = [pl.BlockSpec((1, t, d), lambda b, s: (b, bwd_blk(s), 0))]
    out_shape = [jax.ShapeDtypeStruct((bt, length, d), BF16)]
    if need_ctx:
        out_specs.insert(0, pl.BlockSpec((1, n_c, d), lambda b, s: (b, 0, 0)))
        out_shape.insert(0, jax.ShapeDtypeStruct((bt, n_c, d), BF16))
    rows = max(t, n_c)
    scratch = [pltpu.VMEM((length, d), F32), pltpu.VMEM((n_c, d), F32),
               pltpu.VMEM((h, HG_DK, HG_DK), F32), pltpu.VMEM((h, HG_DK, HG_DK), F32),
               pltpu.VMEM((h, HG_DK, HG_DK), F32),
               pltpu.VMEM((2, c, d), BF16), pltpu.VMEM((2, c, d), BF16), pltpu.VMEM((2, c, d), BF16),
               pltpu.VMEM((2, 1, d), F32), pltpu.VMEM((1, d), F32),
               pltpu.VMEM((c, HG_DK), F32), pltpu.VMEM((c, HG_DK), F32), pltpu.VMEM((c, HG_DK), F32)]
    res = pl.pallas_call(
        functools.partial(_hg_kernel, need_ctx=need_ctx, chunk=c, nblk=nblk),
        grid=(bt, 2 * nblk),
        in_specs=ctx_specs + lat_specs + [vec, vec],
        out_specs=out_specs, out_shape=out_shape, scratch_shapes=scratch,
        compiler_params=_cparams("parallel", "arbitrary"),
        name="hgrn2_scan",
    )(p_c, p_c, p_c, f_c, p_l, p_l, p_l, f_l, lb.reshape(1, d), gn.reshape(1, d))
    return (res[0], res[1]) if need_ctx else (None, res[0])


def _dot_terms(x, m):
    hi, mid, lo = _bf16_terms(x)
    return _dot(hi, m) + (_dot(mid, m) + _dot(lo, m))


def _m2_kernel(zc_ref, xc_ref, bc_ref, cc_ref, dtc_ref, zl_ref, xl_ref, bl_ref, cl_ref, dtl_ref,
               cwx_ref, cwb_ref, cwc_ref, cbx_ref, cbb_ref, cbc_ref,
               bias_ref, aneg_ref, ex_ref, ede_ref, dm_ref, dskip_ref, gn_ref, *rest, need_ctx, chunk):
    if need_ctx:
        yc_ref, yl_ref = rest[:2]
        rest = rest[2:]
    else:
        yc_ref, yl_ref = None, rest[0]
        rest = rest[1:]
    xl_s, bl_s, cl_s, xc_s, bc_s, cc_s, ol_s, oc_s, st_s, p_s, de_s, pc_s, dtt_s, nb_s, dec_s = rest
    c = chunk
    n_lat = xl_ref.shape[1] // c
    n_ctx = xc_ref.shape[1] // c
    hpg = xl_ref.shape[-1] // M2_HEADDIM
    gw = hpg * M2_HEADDIM
    row = lax.broadcasted_iota(jnp.int32, (c, c), 0)
    col = lax.broadcasted_iota(jnp.int32, (c, c), 1)
    lower = row >= col
    upper = row <= col
    tril = lower.astype(BF16)
    triu = upper.astype(BF16)
    ridx = lax.broadcasted_iota(jnp.int32, (c, 1), 0)
    lane = lax.broadcasted_iota(jnp.int32, (1, LANES), 1)
    fwd_lane = lane < hpg
    nd = 2 * hpg
    l_mid, l_lo, l_dt, l_e, l_w, l_one = (k * nd for k in range(1, 7))

    def conv_silu(src_ref, dst_s, w_ref, b_ref, n):
        total = n * c
        halo = 16

        def step(i, carry):
            start = pl.multiple_of(i * c, c)
            rows = pl.ds(start, c)
            x = src_ref[0, rows, :].astype(F32)
            prev_blk = src_ref[0, pl.ds(pl.multiple_of(jnp.maximum(start - halo, 0), halo), halo), :]
            next_blk = src_ref[0, pl.ds(pl.multiple_of(jnp.minimum(start + c, total - halo), halo), halo), :]
            prev_row = jnp.where(i > 0, prev_blk[halo - 1:halo, :].astype(F32), 0.0)
            next_row = jnp.where(i < n - 1, next_blk[0:1, :].astype(F32), 0.0)
            x_prev = jnp.where(ridx == 0, prev_row, pltpu.roll(x, 1, 0))
            x_next = jnp.where(ridx == c - 1, next_row, pltpu.roll(x, c - 1, 0))
            y = w_ref[0:1, :] * x_prev + w_ref[1:2, :] * x + w_ref[2:3, :] * x_next + b_ref[...]
            dst_s[rows, :] = _silu(y).astype(dst_s.dtype)
            return carry

        lax.fori_loop(0, n, step, 0)

    def prepare(dt_ref):
        def step(i, slot):
            rows = _aligned_rows(i * c, c)
            x = dt_ref[0, rows, :] + bias_ref[0]
            dt = jnp.maximum(x, 0.0) + jnp.log(1.0 + jnp.exp(-jnp.abs(x)))
            hi, mid, lo = _bf16_terms(dt * aneg_ref[0])
            bf = _dot(tril, hi) + (_dot(tril, mid) + _dot(tril, lo))
            bb = _dot(triu, hi) + (_dot(triu, mid) + _dot(triu, lo))
            b = jnp.where(fwd_lane, bf, bb)
            b_end = jnp.where(fwd_lane, bf[c - 1:c, :], bb[0:1, :])
            w = jnp.exp(b_end - b) * dt
            hi, mid, lo = (t.astype(F32) for t in _bf16_terms(b))
            q = jnp.where(lane < l_one + 3, 1.0, 0.0)
            for first, field in ((l_w, w), (l_e, jnp.exp(b)), (l_dt, dt), (l_lo, lo), (l_mid, mid)):
                q = jnp.where(lane < first + nd, pltpu.roll(field, first, 1), q)
            q = jnp.where(lane < nd, hi, q)
            p_s[rows, :] = q.astype(BF16)
            pc_s[slot] = q.astype(BF16)
            qt = q.T
            dtt_s[slot] = qt[l_dt:l_dt + nd, :]
            neg = [jnp.concatenate([-qt[first + r:first + r + 1, :] for r in range(nd)], axis=1)
                   for first in (0, l_mid, l_lo)]
            nb_s[slot] = jnp.concatenate(neg + [jnp.zeros((nd - 3, nd * c), F32)], axis=0).astype(BF16)
            de = _dot_terms(jnp.broadcast_to(jnp.exp(b_end), (8, LANES)), ede_ref[...])[0:1, :]
            de_s[i] = de
            dec_s[slot] = de

        return step

    def forward(dt_ref, x_s, b_s, c_s, o_s, n, with_out):
        prepare_chunk = prepare(dt_ref)

        def step(i, slot):
            rows = _aligned_rows(i * c, c)
            p = pc_s[slot]
            bm, xs = b_s[rows, :], x_s[rows, :]
            sf = st_s[:, :gw]
            if with_out:
                cm = c_s[rows, :]
                diff_mat = jnp.concatenate([dm_ref[0:l_one, :], nb_s[slot], dm_ref[l_one + nd:, :]], axis=0)
                d = _dot(p, diff_mat)
                base = _dot_nt(cm, bm)
                exw = _dot(p, ex_ref[:, :2 * gw])
                o_s[rows, :] = exw[:, :gw] * _dot(cm, sf.astype(BF16))
                dtt = dtt_s[slot]
                for r in range(hpg):
                    rb = hpg + r
                    hd = slice(r * M2_HEADDIM, (r + 1) * M2_HEADDIM)
                    mf = jnp.where(lower, jnp.exp(jnp.minimum(d[:, r * c:(r + 1) * c], 0.0)) * dtt[r:r + 1, :], 0.0)
                    mb = jnp.where(upper, jnp.exp(jnp.minimum(d[:, rb * c:(rb + 1) * c], 0.0)) * dtt[rb:rb + 1, :], 0.0)
                    m = (base * (mf + mb)).astype(BF16)
                    o_s[rows, hd] += _dot(m, x_s[rows, hd])
                wexp = exw[:, gw:]
            else:
                wexp = _dot(p, ex_ref[:, gw:2 * gw])
            xw = xs * wexp.astype(BF16)
            st_s[:, :gw] = dec_s[slot][:, :gw] * sf + _dot_tn(bm, xw)

        def body(ii, carry):
            i = 2 * ii
            prepare_chunk(i + 1, 1)
            step(i, 0)
            prepare_chunk(i + 2, 0)
            step(i + 1, 1)
            return carry

        prepare_chunk(0, 0)
        lax.fori_loop(0, n // 2 - 1, body, 0)
        prepare_chunk(n - 1, 1)
        step(n - 2, 0)
        step(n - 1, 1)

    def backward(x_s, b_s, c_s, z_ref, y_ref, o_s, n, with_out):
        def step(j, carry):
            i = n - 1 - j
            rows = pl.ds(pl.multiple_of(i * c, c), c)
            bm, xs = b_s[rows, :], x_s[rows, :]
            sb = st_s[:, gw:]
            ex = _dot(p_s[rows, :], ex_ref[:, 2 * gw:])
            if with_out:
                y = (o_s[rows, :] + ex[:, :gw] * _dot(c_s[rows, :], sb.astype(BF16))
                     + dskip_ref[...] * xs.astype(F32))
                y = y * _silu(z_ref[0, rows, :].astype(F32))
                y_ref[0, rows, :] = (_rms(y) * gn_ref[...]).astype(y_ref.dtype)
            xw = xs * ex[:, gw:].astype(BF16)
            st_s[:, gw:] = de_s[i][:, gw:] * sb + _dot_tn(bm, xw)
            return carry

        lax.fori_loop(0, n, step, 0, unroll=2)

    conv_silu(xc_ref, xc_s, cwx_ref, cbx_ref, n_ctx)
    conv_silu(bc_ref, bc_s, cwb_ref, cbb_ref, n_ctx)
    conv_silu(cc_ref, cc_s, cwc_ref, cbc_ref, n_ctx)
    conv_silu(xl_ref, xl_s, cwx_ref, cbx_ref, n_lat)
    conv_silu(bl_ref, bl_s, cwb_ref, cbb_ref, n_lat)
    conv_silu(cl_ref, cl_s, cwc_ref, cbc_ref, n_lat)
    st_s[...] = jnp.zeros_like(st_s)
    forward(dtc_ref, xc_s, bc_s, cc_s, oc_s, n_ctx, need_ctx)
    backward(xc_s, bc_s, cc_s, zc_ref, yc_ref, oc_s, n_ctx, need_ctx)
    forward(dtl_ref, xl_s, bl_s, cl_s, ol_s, n_lat, True)
    backward(xl_s, bl_s, cl_s, zl_ref, yl_ref, ol_s, n_lat, True)


def _m2_group_lanes(v, groups):
    lead = v.shape[:-1]
    hpg = v.shape[-1] // (2 * groups)
    v = v.reshape(lead + (2, groups, hpg))
    v = jnp.moveaxis(v, -3, -2).reshape(lead + (groups, 2 * hpg))
    pad = jnp.zeros(lead + (groups, LANES - 2 * hpg), v.dtype)
    return jnp.concatenate([v, pad], axis=-1).reshape(lead + (groups * LANES,))


def _mamba2_scan(p_c, dt_c, p_l, dt_l, conv_w, conv_b, dt_bias, a_neg, d_skip, gn, need_ctx):
    bt, length, _ = p_l.shape
    n_c = p_c.shape[1]
    dinner = gn.shape[0]
    heads = dinner // M2_HEADDIM
    g = M2_GROUPS
    hpg = heads // g
    gw = dinner // g
    ns = M2_DSTATE
    conv_dim = conv_w.shape[1]
    zb = dinner // gw
    xb = 2 * dinner // ns

    bias_sel = _m2_group_lanes(dt_bias.astype(F32).reshape(1, -1), g).reshape(g, 1, LANES)
    aneg_sel = _m2_group_lanes(a_neg.astype(F32).reshape(1, -1), g).reshape(g, 1, LANES)
    head_of_lane = jnp.arange(gw) // M2_HEADDIM

    def spread(first_lane):
        return (jnp.arange(LANES)[:, None] == first_lane + head_of_lane[None, :]).astype(BF16)

    nd = 2 * hpg
    l_mid, l_lo, l_e, l_w, l_one = nd, 2 * nd, 4 * nd, 5 * nd, 6 * nd
    ex = jnp.concatenate([spread(l_e), spread(l_w), spread(l_e + hpg), spread(l_w + hpg)], axis=1)
    ede = jnp.concatenate([spread(0), spread(hpg)], axis=1)
    col_head = jnp.arange(nd * M2_CHUNK) // M2_CHUNK
    rows_idx = jnp.arange(LANES)[:, None]
    diff_const = ((rows_idx < l_lo + nd) & (rows_idx % nd == col_head[None, :])).astype(BF16)
    d_exp = jnp.repeat(d_skip.astype(F32), M2_HEADDIM).reshape(1, dinner)

    def cols(rows, width, off):
        return pl.BlockSpec((1, rows, width), lambda b, j: (b, 0, off + j))

    def specs(rows):
        return [cols(rows, gw, 0), cols(rows, gw, zb), cols(rows, ns, xb), cols(rows, ns, xb + g),
                pl.BlockSpec((1, rows, LANES), lambda b, j: (b, 0, j), pipeline_mode=pl.Buffered(1))]

    def cvec(nrows, width, off):
        return pl.BlockSpec((nrows, width), lambda b, j: (0, off + j))

    per_group = pl.BlockSpec((1, 1, LANES), lambda b, j: (j, 0, 0))
    in_specs = (specs(n_c) + specs(length)
                + [cvec(3, gw, 0), cvec(3, ns, dinner // ns), cvec(3, ns, dinner // ns + g),
                   cvec(1, gw, 0), cvec(1, ns, dinner // ns), cvec(1, ns, dinner // ns + g),
                   per_group, per_group,
                   _resident(ex.shape), _resident(ede.shape), _resident(diff_const.shape),
                   cvec(1, gw, 0), cvec(1, gw, 0)])
    out_specs = [cols(length, gw, 0)]
    out_shape = [jax.ShapeDtypeStruct((bt, length, dinner), BF16)]
    if need_ctx:
        out_specs.insert(0, cols(n_c, gw, 0))
        out_shape.insert(0, jax.ShapeDtypeStruct((bt, n_c, dinner), BF16))
    scratch = [pltpu.VMEM((length, gw), BF16), pltpu.VMEM((length, ns), BF16), pltpu.VMEM((length, ns), BF16),
               pltpu.VMEM((n_c, gw), BF16), pltpu.VMEM((n_c, ns), BF16), pltpu.VMEM((n_c, ns), BF16),
               pltpu.VMEM((length, gw), F32), pltpu.VMEM((n_c, gw), F32),
               pltpu.VMEM((ns, 2 * gw), F32),
               pltpu.VMEM((length, LANES), BF16), pltpu.VMEM((length // M2_CHUNK, 1, 2 * gw), F32),
               pltpu.VMEM((2, M2_CHUNK, LANES), BF16), pltpu.VMEM((2, nd, M2_CHUNK), F32),
               pltpu.VMEM((2, nd, nd * M2_CHUNK), BF16), pltpu.VMEM((2, 1, 2 * gw), F32)]
    cb = conv_b.reshape(1, conv_dim)
    res = pl.pallas_call(
        functools.partial(_m2_kernel, need_ctx=need_ctx, chunk=M2_CHUNK),
        grid=(bt, g),
        in_specs=in_specs, out_specs=out_specs, out_shape=out_shape, scratch_shapes=scratch,
        compiler_params=_cparams("parallel", "parallel"),
        name="mamba2_scan",
    )(p_c, p_c, p_c, p_c, dt_c, p_l, p_l, p_l, p_l, dt_l,
      conv_w, conv_w, conv_w, cb, cb, cb, bias_sel, aneg_sel, ex, ede, diff_const, d_exp,
      gn.reshape(1, dinner))
    return (res[0], res[1]) if need_ctx else (None, res[0])


def kernel(x, c, ctx, c_ctx, ada_w, ada_b, norm_g, ret_w_in, ret_w_out, ret_decay, ret_gn, hg_w_in, hg_w_out, hg_lb, hg_gn, m2_w_in, m2_w_out, m2_conv_w, m2_conv_b, m2_dt_bias, m2_a_log, m2_d, m2_gn, ffn_w_up, ffn_conv_w, ffn_conv_b, ffn_w_down):
    bt, length, d = x.shape
    n_c = ctx.shape[1]
    depth = ada_w.shape[0]
    hidden = ffn_conv_w.shape[-1]
    tm = min(512, length)
    tm_c = n_c

    rows = -(-(bt + 1) // 8) * 8
    cc = jnp.concatenate([c, c_ctx[None, :], jnp.zeros((rows - bt - 1, d), F32)], axis=0)
    mod = _modulation(cc, ada_w, ada_b)

    lb_cum = jnp.cumsum(jax.nn.softmax(hg_lb.astype(F32), axis=0), axis=0)
    lb_all = lb_cum - lb_cum[0]

    h_l, h_c = x, ctx
    for i in range(depth):
        kind, j = i % N_MIXERS, i // N_MIXERS
        need_ctx = i < depth - 1
        mod_l = mod[i, :bt].reshape(bt, 6, 1, d)
        mod_c = mod[i, bt].reshape(1, 6, 1, d)
        sh1_l, sc1_l, g1_l, sh2_l, sc2_l, g2_l = (mod_l[:, k] for k in range(6))
        sh1_c, sc1_c, g1_c, sh2_c, sc2_c, g2_c = (mod_c[:, k] for k in range(6))

        if kind == 0:
            w_in = ret_w_in[j].astype(BF16)
            p_l = _project(h_l, norm_g[i, 0], sc1_l, sh1_l, w_in, tm=tm)
            p_c = _project(h_c, norm_g[i, 0], sc1_c, sh1_c, w_in, tm=tm_c)
            log_gamma = jax.nn.log_sigmoid(ret_decay[j].astype(F32))
            y_c, y_l = _retention_scan(p_c, p_l, log_gamma, ret_gn[j], need_ctx)
            w_out = ret_w_out[j].astype(BF16)
        elif kind == 1:
            wq, wff, wfb, wi, wg = jnp.split(hg_w_in[j], 5, axis=-1)
            w_main = jnp.concatenate([wq, wi, wg], axis=-1).astype(BF16)
            w_gate = jnp.concatenate([wff, wfb], axis=-1).astype(BF16)
            p_l, f_l = _project(h_l, norm_g[i, 0], sc1_l, sh1_l, w_main, w_gate, tm=tm)
            p_c, f_c = _project(h_c, norm_g[i, 0], sc1_c, sh1_c, w_main, w_gate, tm=tm_c)
            y_c, y_l = _hgrn2_scan(p_c, f_c, p_l, f_l, lb_all[i], hg_gn[j], need_ctx)
            w_out = hg_w_out[j].astype(BF16)
        else:
            dinner = m2_gn.shape[-1]
            n_main = dinner + m2_conv_w.shape[-1]
            w_main = m2_w_in[j][:, :n_main].astype(BF16)
            w_dt = _m2_group_lanes(m2_w_in[j][:, n_main:], M2_GROUPS).astype(BF16)
            p_l, dt_l = _project(h_l, norm_g[i, 0], sc1_l, sh1_l, w_main, w_dt, tm=tm)
            p_c, dt_c = _project(h_c, norm_g[i, 0], sc1_c, sh1_c, w_main, w_dt, tm=tm_c)
            a_neg = -jnp.exp(m2_a_log[j].astype(F32))
            y_c, y_l = _mamba2_scan(p_c, dt_c, p_l, dt_l, m2_conv_w[j], m2_conv_b[j], m2_dt_bias[j], a_neg,
                                    m2_d[j], m2_gn[j], need_ctx)
            w_out = m2_w_out[j].astype(BF16)

        wa = ffn_w_up[i][:, :hidden].astype(BF16)
        wv = ffn_w_up[i][:, hidden:].astype(BF16)
        wd = ffn_w_down[i].astype(BF16)
        h_l = _out_project(y_l, w_out, h_l, norm_g[i, 1], g1_l, tm=min(2 * tm, length))
        h_l = _conv_ffn(h_l, norm_g[i, 2], sc2_l, sh2_l, wa, wv, ffn_conv_w[i], ffn_conv_b[i], wd,
                        norm_g[i, 3], g2_l, tm=tm, seg=GRID_W)
        if need_ctx:
            h_c = _out_project(y_c, w_out, h_c, norm_g[i, 1], g1_c, tm=tm_c)
            h_c = _conv_ffn(h_c, norm_g[i, 2], sc2_c, sh2_c, wa, wv, ffn_conv_w[i], ffn_conv_b[i], wd,
                            norm_g[i, 3], g2_c, tm=tm_c, seg=n_c)
    return h_l
```

```python
import functools
import math

import jax
import jax.numpy as jnp
from jax import lax
from jax.experimental import pallas as pl
from jax.experimental.pallas import tpu as pltpu

F32 = jnp.float32
BF16 = jnp.bfloat16

NORM_EPS = 1e-6
GRID_W = 64
ROPE_BASE = 10000.0
N_MIXERS = 3

RET_HEADS = 4
RET_CHUNK = 256
HG_DK = 128
HG_CHUNK = 64
HG_BLOCK = 1024
HG_SAFE_LOG_DECAY = -80.0
M2_HEADDIM = 64
M2_GROUPS = 4
M2_DSTATE = 128
M2_CHUNK = 128
LANES = 128

V7X_VMEM_LIMIT_BYTES = 56 * 1024 * 1024

_HI = lax.Precision.HIGHEST


def _cparams(*sem):
    return pltpu.CompilerParams(dimension_semantics=sem, vmem_limit_bytes=V7X_VMEM_LIMIT_BYTES)


def _sigmoid(x):
    return 1.0 / (1.0 + jnp.exp(-x))


def _silu(x):
    return x * _sigmoid(x)


def _rms(t):
    return t * lax.rsqrt(jnp.mean(t * t, axis=-1, keepdims=True) + NORM_EPS)


def _dot(a, b):
    return jnp.dot(a, b, preferred_element_type=F32)


def _dot_nt(a, b):
    return lax.dot_general(a, b, (((1,), (1,)), ((), ())), preferred_element_type=F32)


def _dot_tn(a, b):
    return lax.dot_general(a, b, (((0,), (0,)), ((), ())), preferred_element_type=F32)


def _aligned_rows(start, size):
    return pl.ds(start if isinstance(start, int) else pl.multiple_of(start, size), size)


def _resident(shape):
    return pl.BlockSpec(shape, lambda *_: (0,) * len(shape), pipeline_mode=pl.Buffered(1))


def _mod_kernel(c_ref, w_ref, b_ref, o_ref):
    s = _silu(c_ref[...]).astype(BF16)
    o_ref[0] = _dot(s, w_ref[0].astype(BF16)) + b_ref[0]


def _modulation(cc, ada_w, ada_b):
    depth, d, n = ada_w.shape
    rows = cc.shape[0]
    tn = 1536
    return pl.pallas_call(
        _mod_kernel,
        grid=(depth, n // tn),
        in_specs=[pl.BlockSpec((rows, d), lambda i, j: (0, 0)),
                  pl.BlockSpec((1, d, tn), lambda i, j: (i, 0, j)),
                  pl.BlockSpec((1, 1, tn), lambda i, j: (i, 0, j))],
        out_specs=pl.BlockSpec((1, rows, tn), lambda i, j: (i, 0, j)),
        out_shape=jax.ShapeDtypeStruct((depth, rows, n), F32),
        compiler_params=_cparams("parallel", "parallel"),
        name="adaln_mod",
    )(cc, ada_w, ada_b.reshape(depth, 1, n))


def _norm_mod(x, g, sc, sh):
    return _rms(x) * (g * (1.0 + sc)) + sh


def _proj_kernel(x_ref, g_ref, sc_ref, sh_ref, w_ref, *rest, ncol):
    u = _norm_mod(x_ref[0], g_ref[...], sc_ref[0], sh_ref[0]).astype(BF16)
    if len(rest) == 3:
        w2_ref, o_ref, o2_ref = rest
        o2_ref[0] = _dot(u, w2_ref[...])
    else:
        (o_ref,) = rest
    n = w_ref.shape[1]
    for n0 in range(0, n, ncol):
        n1 = min(n0 + ncol, n)
        o_ref[0, :, n0:n1] = _dot(u, w_ref[:, n0:n1]).astype(o_ref.dtype)


def _project(h, g, sc, sh, w, w2=None, *, tm):
    bt, length, d = h.shape
    n = w.shape[1]
    per_batch = sc.shape[0] != 1
    mod_spec = pl.BlockSpec((1, 1, d), (lambda b, i: (b, 0, 0)) if per_batch else (lambda b, i: (0, 0, 0)))
    in_specs = [pl.BlockSpec((1, tm, d), lambda b, i: (b, i, 0)),
                pl.BlockSpec((1, d), lambda b, i: (0, 0)),
                mod_spec, mod_spec, _resident((d, n))]
    out_specs = [pl.BlockSpec((1, tm, n), lambda b, i: (b, i, 0))]
    out_shape = [jax.ShapeDtypeStruct((bt, length, n), BF16)]
    args = [h, g.reshape(1, d), sc, sh, w]
    if w2 is not None:
        n2 = w2.shape[1]
        in_specs.append(_resident((d, n2)))
        out_specs.append(pl.BlockSpec((1, tm, n2), lambda b, i: (b, i, 0)))
        out_shape.append(jax.ShapeDtypeStruct((bt, length, n2), F32))
        args.append(w2)
    res = pl.pallas_call(
        functools.partial(_proj_kernel, ncol=512),
        grid=(bt, length // tm),
        in_specs=in_specs, out_specs=out_specs, out_shape=out_shape,
        compiler_params=_cparams("parallel", "parallel"),
        name="norm_mod_proj",
    )(*args)
    return res if w2 is not None else res[0]


def _out_kernel(y_ref, w_ref, h_ref, ng_ref, gate_ref, o_ref):
    t = _dot(y_ref[0], w_ref[...])
    o_ref[0] = h_ref[0] + gate_ref[0] * (_rms(t) * ng_ref[...])


def _out_project(y, w, h, ng, gate, *, tm):
    bt, length, d = h.shape
    dy = y.shape[-1]
    per_batch = gate.shape[0] != 1
    gate_spec = pl.BlockSpec((1, 1, d), (lambda b, i: (b, 0, 0)) if per_batch else (lambda b, i: (0, 0, 0)))
    return pl.pallas_call(
        _out_kernel,
        grid=(bt, length // tm),
        in_specs=[pl.BlockSpec((1, tm, dy), lambda b, i: (b, i, 0)),
                  _resident((dy, d)),
                  pl.BlockSpec((1, tm, d), lambda b, i: (b, i, 0)),
                  pl.BlockSpec((1, d), lambda b, i: (0, 0)),
                  gate_spec],
        out_specs=pl.BlockSpec((1, tm, d), lambda b, i: (b, i, 0)),
        out_shape=jax.ShapeDtypeStruct((bt, length, d), F32),
        compiler_params=_cparams("parallel", "parallel"),
        name="out_proj_residual",
    )(y, w, h, ng.reshape(1, d), gate)


def _gelu_tanh(x):
    return 0.5 * x * (1.0 + jnp.tanh(math.sqrt(2.0 / math.pi) * (x + 0.044715 * (x * x * x))))


def _ffn_kernel(h_ref, g_ref, sc_ref, sh_ref, wa_ref, wv_ref, cw_ref, cb_ref, wd_ref, ng_ref, gate_ref, o_ref,
                *, seg, hidden_chunks):
    tile = h_ref.shape[1]
    parts = 2 if (tile // 2) % seg == 0 else 1
    tm = tile // parts
    pos = lax.broadcasted_iota(jnp.int32, (tm, 1), 0) % seg
    has_prev = pos != 0
    has_next = pos != seg - 1
    for part_idx in range(parts):
        rows = slice(part_idx * tm, (part_idx + 1) * tm)
        h = h_ref[0, rows, :]
        u = _norm_mod(h, g_ref[...], sc_ref[0], sh_ref[0]).astype(BF16)
        acc = None
        for c0, c1 in hidden_chunks:
            a = _dot(u, wa_ref[:, c0:c1])
            v = _dot(u, wv_ref[:, c0:c1])
            a_prev = jnp.where(has_prev, pltpu.roll(a, 1, 0), 0.0)
            a_next = jnp.where(has_next, pltpu.roll(a, tm - 1, 0), 0.0)
            ac = (cw_ref[0:1, c0:c1] * a_prev + cw_ref[1:2, c0:c1] * a + cw_ref[2:3, c0:c1] * a_next
                  + cb_ref[:, c0:c1])
            hid = (_gelu_tanh(ac) * v).astype(BF16)
            part = _dot(hid, wd_ref[c0:c1, :])
            acc = part if acc is None else acc + part
        o_ref[0, rows, :] = h + gate_ref[0] * (_rms(acc) * ng_ref[...])


def _hidden_chunks(hidden, size):
    return tuple((c0, min(c0 + size, hidden)) for c0 in range(0, hidden, size))


def _conv_ffn(h, g, sc, sh, wa, wv, cw, cb, wd, ng, gate, *, tm, seg):
    bt, length, d = h.shape
    hidden = wa.shape[1]
    per_batch = sc.shape[0] != 1
    mod_spec = pl.BlockSpec((1, 1, d), (lambda b, i: (b, 0, 0)) if per_batch else (lambda b, i: (0, 0, 0)))
    vec_d = pl.BlockSpec((1, d), lambda b, i: (0, 0))
    return pl.pallas_call(
        functools.partial(_ffn_kernel, seg=seg, hidden_chunks=_hidden_chunks(hidden, 2816)),
        grid=(bt, length // tm),
        in_specs=[pl.BlockSpec((1, tm, d), lambda b, i: (b, i, 0)),
                  vec_d, mod_spec, mod_spec,
                  _resident((d, hidden)), _resident((d, hidden)),
                  pl.BlockSpec((3, hidden), lambda b, i: (0, 0)),
                  pl.BlockSpec((1, hidden), lambda b, i: (0, 0)),
                  _resident((hidden, d)),
                  vec_d, mod_spec],
        out_specs=pl.BlockSpec((1, tm, d), lambda b, i: (b, i, 0)),
        out_shape=jax.ShapeDtypeStruct((bt, length, d), F32),
        compiler_params=_cparams("parallel", "parallel"),
        name="conv_glu_ffn",
    )(h, g.reshape(1, d), sc, sh, wa, wv, cw, cb.reshape(1, hidden), wd, ng.reshape(1, d), gate)


def _ret_kernel(lg_ref, qc_ref, kc_ref, vc_ref, gc_ref, ql_ref, kl_ref, vl_ref, gl_ref, gn_ref,
                rcos_ref, rsin_ref, ccos_ref, csin_ref, *rest, need_ctx, chunk):
    if need_ctx:
        yc_ref, yl_ref, qr_s, kr_s, qcs_s, kcs_s, ol_s, oc_s, sf_s, sb_s = rest
    else:
        yl_ref, qr_s, kr_s, qcs_s, kcs_s, ol_s, oc_s, sf_s, sb_s = rest
        yc_ref = None
    c = chunk
    head = pl.program_id(1)
    lgf = lg_ref[0, head]
    lgb = lg_ref[1, head]
    dk = ql_ref.shape[-1]
    k_scale = dk ** -0.5
    n_lat = ql_ref.shape[1] // c
    n_ctx = qc_ref.shape[1] // c

    ti = lax.broadcasted_iota(jnp.int32, (c, 1), 0).astype(F32)
    dq_f = jnp.exp(lgf * (ti + 1.0))
    dk_f = jnp.exp(lgf * (c - 1.0 - ti))
    dq_b = jnp.exp(lgb * (c - ti))
    dk_b = jnp.exp(lgb * ti)
    full_chunk = jnp.full((1, 1), float(c), F32)
    dc_f = jnp.exp(lgf * full_chunk)
    dc_b = jnp.exp(lgb * full_chunk)
    diff = (lax.broadcasted_iota(jnp.int32, (c, c), 0) - lax.broadcasted_iota(jnp.int32, (c, c), 1)).astype(F32)
    dmask = (jnp.where(diff >= 0, jnp.exp(lgf * jnp.maximum(diff, 0.0)), 0.0)
             + jnp.where(diff <= 0, jnp.exp(lgb * jnp.maximum(-diff, 0.0)), 0.0))

    half = dk // 2

    def rope_row(r, carry):
        rows = pl.ds(pl.multiple_of(r * GRID_W, GRID_W), GRID_W)
        cr = rcos_ref[pl.ds(r, 1), :]
        sr = rsin_ref[pl.ds(r, 1), :]
        for src, dst, scale in ((ql_ref, qr_s, 1.0), (kl_ref, kr_s, k_scale)):
            t = src[0, rows, :].astype(F32) * scale
            t0, t1 = t[:, :half], t[:, half:]
            dst[rows, :half] = (t0 * cr + pltpu.roll(t0, half // 2, 1) * sr).astype(BF16)
            dst[rows, half:] = (t1 * ccos_ref[...] + pltpu.roll(t1, half // 2, 1) * csin_ref[...]).astype(BF16)
        return carry

    lax.fori_loop(0, ql_ref.shape[1] // GRID_W, rope_row, 0, unroll=4)
    qcs_s[...] = qc_ref[0]
    kcs_s[...] = (kc_ref[0].astype(F32) * k_scale).astype(BF16)

    sf_s[...] = jnp.zeros_like(sf_s)
    sb_s[...] = jnp.zeros_like(sb_s)

    def scan(q_s, k_s, v_ref, g_ref, y_ref, o_s, n):
        with_out = y_ref is not None

        def chunk(i):
            return pl.ds(i * c if isinstance(i, int) else pl.multiple_of(i * c, c), c)

        def forward(rows):
            qf, kf, vf = q_s[rows, :], k_s[rows, :], v_ref[0, rows, :]
            o = None
            if with_out:
                p = (_dot_nt(qf, kf) * dmask).astype(BF16)
                o = _dot(p, vf) + dq_f * _dot(qf, sf_s[...].astype(BF16))
            kd = (kf.astype(F32) * dk_f).astype(BF16)
            sf_s[...] = dc_f * sf_s[...] + _dot_tn(kd, vf)
            return o

        def backward(rows):
            qb, kb, vb = q_s[rows, :], k_s[rows, :], v_ref[0, rows, :]
            o = dq_b * _dot(qb, sb_s[...].astype(BF16)) if with_out else None
            kdb = (kb.astype(F32) * dk_b).astype(BF16)
            sb_s[...] = dc_b * sb_s[...] + _dot_tn(kdb, vb)
            return o

        def finish(rows, o):
            o = o - jnp.mean(o, axis=-1, keepdims=True)
            o = _rms(o) * gn_ref[...]
            y_ref[0, rows, :] = (_silu(g_ref[0, rows, :].astype(F32)) * o).astype(y_ref.dtype)

        def first_half(i, carry):
            fw, bw = chunk(i), chunk(n - 1 - i)
            of, ob = forward(fw), backward(bw)
            if with_out:
                o_s[fw, :] = of
                o_s[bw, :] = ob
            return carry

        def second_half(i, carry):
            fw, bw = chunk(i), chunk(n - 1 - i)
            of, ob = forward(fw), backward(bw)
            if with_out:
                finish(fw, o_s[fw, :] + of)
                finish(bw, o_s[bw, :] + ob)
            return carry

        lax.fori_loop(0, n // 2, first_half, 0)
        if n % 2:
            mid = chunk(n // 2)
            of, ob = forward(mid), backward(mid)
            if with_out:
                finish(mid, of + ob)
        lax.fori_loop(n - n // 2, n, second_half, 0)

    scan(qcs_s, kcs_s, vc_ref, gc_ref, yc_ref, oc_s, n_ctx)
    scan(qr_s, kr_s, vl_ref, gl_ref, yl_ref, ol_s, n_lat)


def _rope_tables(length, dk):
    nf = dk // 4
    inv = ROPE_BASE ** (-jnp.arange(nf, dtype=F32) / nf)

    def tab(p):
        ang = p[:, None] * inv
        cos, sin = jnp.cos(ang), jnp.sin(ang)
        return jnp.concatenate([cos, cos], -1), jnp.concatenate([-sin, sin], -1)

    rcos, rsin = tab(jnp.arange(length // GRID_W).astype(F32))
    ccos, csin = tab(jnp.arange(GRID_W).astype(F32))
    return rcos, rsin, ccos, csin


def _retention_scan(p_c, p_l, log_gamma, gn, need_ctx):
    bt, length, _ = p_l.shape
    n_c = p_c.shape[1]
    hv = gn.shape[0]
    dv = hv // RET_HEADS
    dk = dv // 2
    h = RET_HEADS
    tables = _rope_tables(length, dk)

    def cols(rows, width, off):
        return pl.BlockSpec((1, rows, width), lambda b, j: (b, 0, off + j))

    def specs(rows):
        return [cols(rows, dk, 0), cols(rows, dk, h), cols(rows, dv, h), cols(rows, dv, 2 * h)]

    in_specs = ([pl.BlockSpec(memory_space=pltpu.SMEM)] + specs(n_c) + specs(length)
                + [pl.BlockSpec((1, dv), lambda b, j: (0, j))]
                + [_resident(t.shape) for t in tables])
    out_specs = [cols(length, dv, 0)]
    out_shape = [jax.ShapeDtypeStruct((bt, length, hv), BF16)]
    if need_ctx:
        out_specs.insert(0, cols(n_c, dv, 0))
        out_shape.insert(0, jax.ShapeDtypeStruct((bt, n_c, hv), BF16))
    scratch = [pltpu.VMEM((length, dk), BF16), pltpu.VMEM((length, dk), BF16),
               pltpu.VMEM((n_c, dk), BF16), pltpu.VMEM((n_c, dk), BF16),
               pltpu.VMEM((length, dv), F32), pltpu.VMEM((n_c, dv), F32),
               pltpu.VMEM((dk, dv), F32), pltpu.VMEM((dk, dv), F32)]
    res = pl.pallas_call(
        functools.partial(_ret_kernel, need_ctx=need_ctx, chunk=RET_CHUNK),
        grid=(bt, h),
        in_specs=in_specs, out_specs=out_specs, out_shape=out_shape, scratch_shapes=scratch,
        compiler_params=_cparams("parallel", "parallel"),
        name="retention_scan",
    )(log_gamma, p_c, p_c, p_c, p_c, p_l, p_l, p_l, p_l, gn.reshape(1, hv), *tables)
    return (res[0], res[1]) if need_ctx else (None, res[0])


def _bf16_terms(x):
    hi = x.astype(BF16)
    r1 = x - hi.astype(F32)
    mid = r1.astype(BF16)
    lo = (r1 - mid.astype(F32)).astype(BF16)
    return hi, mid, lo


def _hg_kernel(qc_ref, ic_ref, gc_ref, fc_ref, ql_ref, il_ref, gl_ref, fl_ref, lb_ref, gn_ref, *rest,
               need_ctx, chunk, nblk):
    if need_ctx:
        yc_ref, yl_ref = rest[:2]
        rest = rest[2:]
    else:
        yc_ref, yl_ref = None, rest[0]
        rest = rest[1:]
    ol_s, oc_s, st_f, st_b, sv_s, qd_s, kd_s, kl_s, eb_s, mn_s, tb_s, tk_s, tv_s = rest
    c = chunk
    step = pl.program_id(1)
    d = lb_ref.shape[-1]
    heads = d // HG_DK
    t = ql_ref.shape[1]
    n_lat = t // c
    n_ctx = qc_ref.shape[1] // c
    row = lax.broadcasted_iota(jnp.int32, (c, c), 0)
    col = lax.broadcasted_iota(jnp.int32, (c, c), 1)
    lower = row >= col
    upper = row <= col
    tidx = lax.broadcasted_iota(jnp.int32, (c, 1), 0)

    def gates(f, lb):
        e = jnp.exp(-jnp.abs(f))
        r = 1.0 / (1.0 + e)
        pos = f >= 0
        sig = jnp.where(pos, r, e * r)
        sig_neg = jnp.where(pos, e * r, r)
        return jnp.log(lb + (1.0 - lb) * sig), (1.0 - lb) * sig_neg

    def cumsum(mask, la):
        tri = mask.astype(BF16)
        hi, mid, lo = _bf16_terms(la)
        return _dot(tri, hi) + (_dot(tri, mid) + _dot(tri, lo))

    def sweep(q_ref, i_ref, f_ref, fcols, n, mask, last, st, reverse, emit):
        lb = lb_ref[...]

        def chunk_rows(j):
            i = (n - 1 - j) if reverse else j
            return i, _aligned_rows(i * c, c)

        def prepare(j, slot):
            _, rows = chunk_rows(j)
            la, kk = gates(f_ref[0, rows, fcols], lb)
            b = cumsum(mask, la)
            b_end = b[last:last + 1, :]
            qd_s[slot] = (_silu(q_ref[0, rows, :].astype(F32)) * jnp.exp(b)).astype(BF16)
            kd = kk * jnp.exp(-b)
            e_end = jnp.exp(b_end)
            kd_s[slot] = kd.astype(BF16)
            kl_s[slot] = (kd * e_end).astype(BF16)
            eb_s[slot] = e_end
            mn_s[...] = jnp.minimum(mn_s[...], b_end)

        def fast(j, slot):
            _, rows = chunk_rows(j)
            for h in range(heads):
                hc = slice(h * HG_DK, (h + 1) * HG_DK)
                kl, v = kl_s[slot, :, hc], i_ref[0, rows, hc]
                if emit is not None:
                    qd = qd_s[slot, :, hc]
                    s = jnp.where(mask, _dot_nt(qd, kd_s[slot, :, hc]), 0.0).astype(BF16)
                    emit(rows, hc, _dot(s, v) + _dot_nt(qd, st[h].astype(BF16)))
                st[h] = eb_s[slot, :, hc] * st[h] + _dot_tn(v, kl)

        sv_s[...] = st[...]
        mn_s[...] = jnp.zeros_like(mn_s)
        prepare(0, 0)

        def body(jj, carry):
            j = 2 * jj
            prepare(j + 1, 1)
            fast(j, 0)
            prepare(j + 2, 0)
            fast(j + 1, 1)
            return carry

        lax.fori_loop(0, n // 2 - 1, body, 0)
        prepare(n - 1, 1)
        fast(n - 2, 0)
        fast(n - 1, 1)

        @pl.when(jnp.min(mn_s[...]) < HG_SAFE_LOG_DECAY)
        def _exact():
            st[...] = sv_s[...]

            def body(j, carry):
                i, rows = chunk_rows(j)
                for h in range(heads):
                    hc = slice(h * HG_DK, (h + 1) * HG_DK)
                    fh = f_ref[0, rows, hc] if fcols == slice(None) else f_ref[0, rows, fcols.start + h * HG_DK:
                                                                               fcols.start + (h + 1) * HG_DK]
                    la, kk = gates(fh, lb_ref[:, hc])
                    b = cumsum(mask, la)
                    b_end = b[last:last + 1, :]
                    v = i_ref[0, rows, hc]
                    if emit is not None:
                        qs = _silu(q_ref[0, rows, hc].astype(F32))
                        tb_s[...] = b
                        tk_s[...] = kk
                        tv_s[...] = v.astype(F32)

                        def inner(s, acc):
                            one = pl.ds(s, 1)
                            dec = jnp.exp(jnp.minimum(b - tb_s[one, :], 0.0))
                            w = jnp.sum(qs * tk_s[one, :] * dec, axis=-1, keepdims=True)
                            keep = (tidx <= s) if reverse else (tidx >= s)
                            return acc + jnp.where(keep, w, 0.0) * tv_s[one, :]

                        o = lax.fori_loop(0, c, inner, jnp.zeros((c, HG_DK), F32))
                        qd = (qs * jnp.exp(b)).astype(BF16)
                        emit(rows, hc, o + _dot_nt(qd, st[h].astype(BF16)))
                    kl = (kk * jnp.exp(b_end - b)).astype(BF16)
                    st[h] = jnp.exp(b_end) * st[h] + _dot_tn(v, kl)
                return carry

            lax.fori_loop(0, n, body, 0)

    def store_partial(o_s, row0):
        def emit(rows, hc, o):
            o_s[_aligned_rows(row0 + rows.start, c), hc] = o
        return emit

    def finish(o_s, row0, g_ref, y_ref):
        def emit(rows, hc, o):
            total = o_s[_aligned_rows(row0 + rows.start, c), hc] + o
            y = _silu(g_ref[0, rows, hc].astype(F32)) * (_rms(total) * gn_ref[:, hc])
            y_ref[0, rows, hc] = y.astype(y_ref.dtype)
        return emit

    @pl.when(step == 0)
    def _context():
        st_f[...] = jnp.zeros_like(st_f)
        st_b[...] = jnp.zeros_like(st_b)
        sweep(qc_ref, ic_ref, fc_ref, slice(0, d), n_ctx, lower, c - 1, st_f, False,
              store_partial(oc_s, 0) if need_ctx else None)
        sweep(qc_ref, ic_ref, fc_ref, slice(d, 2 * d), n_ctx, upper, 0, st_b, True,
              finish(oc_s, 0, gc_ref, yc_ref) if need_ctx else None)

    @pl.when(step < nblk)
    def _forward():
        sweep(ql_ref, il_ref, fl_ref, slice(None), n_lat, lower, c - 1, st_f, False,
              store_partial(ol_s, step * t))

    @pl.when(step >= nblk)
    def _backward():
        sweep(ql_ref, il_ref, fl_ref, slice(None), n_lat, upper, 0, st_b, True,
              finish(ol_s, (2 * nblk - 1 - step) * t, gl_ref, yl_ref))


def _hgrn2_scan(p_c, f_c, p_l, f_l, lb, gn, need_ctx):
    bt, length, d3 = p_l.shape
    d = d3 // 3
    n_c = p_c.shape[1]
    h = d // HG_DK
    t = min(HG_BLOCK, length)
    nblk = length // t
    c = HG_CHUNK

    def blk(s):
        return jnp.where(s < nblk, s, 2 * nblk - 1 - s)

    def bwd_blk(s):
        return jnp.where(s < nblk, nblk - 1, 2 * nblk - 1 - s)

    lat_specs = [pl.BlockSpec((1, t, d), lambda b, s: (b, blk(s), 0)),
                 pl.BlockSpec((1, t, d), lambda b, s: (b, blk(s), 1)),
                 pl.BlockSpec((1, t, d), lambda b, s: (b, bwd_blk(s), 2)),
                 pl.BlockSpec((1, t, d), lambda b, s: (b, blk(s), jnp.where(s < nblk, 0, 1)))]
    ctx_specs = [pl.BlockSpec((1, n_c, d), lambda b, s: (b, 0, 0)),
                 pl.BlockSpec((1, n_c, d), lambda b, s: (b, 0, 1)),
                 pl.BlockSpec((1, n_c, d), lambda b, s: (b, 0, 2)),
                 pl.BlockSpec((1, n_c, 2 * d), lambda b, s: (b, 0, 0))]
    vec = pl.BlockSpec((1, d), lambda b, s: (0, 0))
    out_specs = [pl.BlockSpec((1, t, d), lambda b, s: (b, bwd_blk(s), 0))]
    out_shape = [jax.ShapeDtypeStruct((bt, length, d), BF16)]
    if need_ctx:
        out_specs.insert(0, pl.BlockSpec((1, n_c, d), lambda b, s: (b, 0, 0)))
        out_shape.insert(0, jax.ShapeDtypeStruct((bt, n_c, d), BF16))
    rows = max(t, n_c)
    scratch = [pltpu.VMEM((length, d), F32), pltpu.VMEM((n_c, d), F32),
               pltpu.VMEM((h, HG_DK, HG_DK), F32), pltpu.VMEM((h, HG_DK, HG_DK), F32),
               pltpu.VMEM((h, HG_DK, HG_DK), F32),
               pltpu.VMEM((2, c, d), BF16), pltpu.VMEM((2, c, d), BF16), pltpu.VMEM((2, c, d), BF16),
               pltpu.VMEM((2, 1, d), F32), pltpu.VMEM((1, d), F32),
               pltpu.VMEM((c, HG_DK), F32), pltpu.VMEM((c, HG_DK), F32), pltpu.VMEM((c, HG_DK), F32)]
    res = pl.pallas_call(
        functools.partial(_hg_kernel, need_ctx=need_ctx, chunk=c, nblk=nblk),
        grid=(bt, 2 * nblk),
        in_specs=ctx_specs + lat_specs + [vec, vec],
        out_specs=out_specs, out_shape=out_shape, scratch_shapes=scratch,
        compiler_params=_cparams("parallel", "arbitrary"),
        name="hgrn2_scan",
    )(p_c, p_c, p_c, f_c, p_l, p_l, p_l, f_l, lb.reshape(1, d), gn.reshape(1, d))
    return (res[0], res[1]) if need_ctx else (None, res[0])


def _dot_terms(x, m):
    hi, mid, lo = _bf16_terms(x)
    return _dot(hi, m) + (_dot(mid, m) + _dot(lo, m))


def _m2_kernel(zc_ref, xc_ref, bc_ref, cc_ref, dtc_ref, zl_ref, xl_ref, bl_ref, cl_ref, dtl_ref,
               cwx_ref, cwb_ref, cwc_ref, cbx_ref, cbb_ref, cbc_ref,
               bias_ref, aneg_ref, ex_ref, ede_ref, dm_ref, dskip_ref, gn_ref, *rest, need_ctx, chunk):
    if need_ctx:
        yc_ref, yl_ref = rest[:2]
        rest = rest[2:]
    else:
        yc_ref, yl_ref = None, rest[0]
        rest = rest[1:]
    xl_s, bl_s, cl_s, xc_s, bc_s, cc_s, ol_s, oc_s, st_s, p_s, de_s, pc_s, dtt_s, nb_s, dec_s = rest
    c = chunk
    n_lat = xl_ref.shape[1] // c
    n_ctx = xc_ref.shape[1] // c
    hpg = xl_ref.shape[-1] // M2_HEADDIM
    gw = hpg * M2_HEADDIM
    row = lax.broadcasted_iota(jnp.int32, (c, c), 0)
    col = lax.broadcasted_iota(jnp.int32, (c, c), 1)
    lower = row >= col
    upper = row <= col
    tril = lower.astype(BF16)
    ridx = lax.broadcasted_iota(jnp.int32, (c, 1), 0)
    lane = lax.broadcasted_iota(jnp.int32, (1, LANES), 1)
    fwd_lane = lane < hpg
    nd = 2 * hpg
    l_mid, l_lo, l_dt, l_e, l_w, l_one = (k * nd for k in range(1, 7))

    def conv_silu(src_ref, dst_s, w_ref, b_ref, n):
        total = n * c
        halo = 16

        def step(i, carry):
            start = pl.multiple_of(i * c, c)
            rows = pl.ds(start, c)
            x = src_ref[0, rows, :].astype(F32)
            prev_blk = src_ref[0, pl.ds(pl.multiple_of(jnp.maximum(start - halo, 0), halo), halo), :]
            next_blk = src_ref[0, pl.ds(pl.multiple_of(jnp.minimum(start + c, total - halo), halo), halo), :]
            prev_row = jnp.where(i > 0, prev_blk[halo - 1:halo, :].astype(F32), 0.0)
            next_row = jnp.where(i < n - 1, next_blk[0:1, :].astype(F32), 0.0)
            x_prev = jnp.where(ridx == 0, prev_row, pltpu.roll(x, 1, 0))
            x_next = jnp.where(ridx == c - 1, next_row, pltpu.roll(x, c - 1, 0))
            y = w_ref[0:1, :] * x_prev + w_ref[1:2, :] * x + w_ref[2:3, :] * x_next + b_ref[...]
            dst_s[rows, :] = _silu(y).astype(dst_s.dtype)
            return carry

        lax.fori_loop(0, n, step, 0)

    def prepare(dt_ref):
        def step(i, slot):
            rows = _aligned_rows(i * c, c)
            x = dt_ref[0, rows, :] + bias_ref[0]
            dt = jnp.maximum(x, 0.0) + jnp.log(1.0 + jnp.exp(-jnp.abs(x)))
            la = dt * aneg_ref[0]
            hi, mid, lo = _bf16_terms(la)
            bf = _dot(tril, hi) + (_dot(tril, mid) + _dot(tril, lo))
            total = bf[c - 1:c, :]
            b = jnp.where(fwd_lane, bf, total - bf + la)
            b_end = total
            w = jnp.exp(b_end - b) * dt
            hi, mid, lo = (t.astype(F32) for t in _bf16_terms(b))
            q = jnp.where(lane < l_one + 3, 1.0, 0.0)
            for first, field in ((l_w, w), (l_e, jnp.exp(b)), (l_dt, dt), (l_lo, lo), (l_mid, mid)):
                q = jnp.where(lane < first + nd, pltpu.roll(field, first, 1), q)
            q = jnp.where(lane < nd, hi, q)
            p_s[rows, :] = q.astype(BF16)
            pc_s[slot] = q.astype(BF16)
            qt = q.T
            dtt_s[slot] = qt[l_dt:l_dt + nd, :]
            neg = [jnp.concatenate([-qt[first + r:first + r + 1, :] for r in range(nd)], axis=1)
                   for first in (0, l_mid, l_lo)]
            nb_s[slot] = jnp.concatenate(neg + [jnp.zeros((nd - 3, nd * c), F32)], axis=0).astype(BF16)
            de = _dot_terms(jnp.broadcast_to(jnp.exp(b_end), (8, LANES)), ede_ref[...])[0:1, :]
            de_s[i] = de
            dec_s[slot] = de

        return step

    def forward(dt_ref, x_s, b_s, c_s, o_s, n, with_out):
        prepare_chunk = prepare(dt_ref)

        def step(i, slot):
            rows = _aligned_rows(i * c, c)
            p = pc_s[slot]
            bm, xs = b_s[rows, :], x_s[rows, :]
            sf = st_s[:, :gw]
            if with_out:
                cm = c_s[rows, :]
                diff_mat = jnp.concatenate([dm_ref[0:l_one, :], nb_s[slot], dm_ref[l_one + nd:, :]], axis=0)
                d = _dot(p, diff_mat)
                base = _dot_nt(cm, bm)
                exw = _dot(p, ex_ref[:, :2 * gw])
                o_s[rows, :] = exw[:, :gw] * _dot(cm, sf.astype(BF16))
                dtt = dtt_s[slot]
                for r in range(hpg):
                    rb = hpg + r
                    hd = slice(r * M2_HEADDIM, (r + 1) * M2_HEADDIM)
                    mf = jnp.where(lower, jnp.exp(jnp.minimum(d[:, r * c:(r + 1) * c], 0.0)) * dtt[r:r + 1, :], 0.0)
                    mb = jnp.where(upper, jnp.exp(jnp.minimum(d[:, rb * c:(rb + 1) * c], 0.0)) * dtt[rb:rb + 1, :], 0.0)
                    m = (base * (mf + mb)).astype(BF16)
                    o_s[rows, hd] += _dot(m, x_s[rows, hd])
                wexp = exw[:, gw:]
            else:
                wexp = _dot(p, ex_ref[:, gw:2 * gw])
            xw = xs * wexp.astype(BF16)
            st_s[:, :gw] = dec_s[slot][:, :gw] * sf + _dot_tn(bm, xw)

        def body(ii, carry):
            i = 2 * ii
            prepare_chunk(i + 1, 1)
            step(i, 0)
            prepare_chunk(i + 2, 0)
            step(i + 1, 1)
            return carry

        prepare_chunk(0, 0)
        lax.fori_loop(0, n // 2 - 1, body, 0)
        prepare_chunk(n - 1, 1)
        step(n - 2, 0)
        step(n - 1, 1)

    def backward(x_s, b_s, c_s, z_ref, y_ref, o_s, n, with_out):
        def step(j, carry):
            i = n - 1 - j
            rows = pl.ds(pl.multiple_of(i * c, c), c)
            bm, xs = b_s[rows, :], x_s[rows, :]
            sb = st_s[:, gw:]
            ex = _dot(p_s[rows, :], ex_ref[:, 2 * gw:])
            if with_out:
                y = (o_s[rows, :] + ex[:, :gw] * _dot(c_s[rows, :], sb.astype(BF16))
                     + dskip_ref[...] * xs.astype(F32))
                y = y * _silu(z_ref[0, rows, :].astype(F32))
                y_ref[0, rows, :] = (_rms(y) * gn_ref[...]).astype(y_ref.dtype)
            xw = xs * ex[:, gw:].astype(BF16)
            st_s[:, gw:] = de_s[i][:, gw:] * sb + _dot_tn(bm, xw)
            return carry

        lax.fori_loop(0, n, step, 0, unroll=2)

    conv_silu(xc_ref, xc_s, cwx_ref, cbx_ref, n_ctx)
    conv_silu(bc_ref, bc_s, cwb_ref, cbb_ref, n_ctx)
    conv_silu(cc_ref, cc_s, cwc_ref, cbc_ref, n_ctx)
    conv_silu(xl_ref, xl_s, cwx_ref, cbx_ref, n_lat)
    conv_silu(bl_ref, bl_s, cwb_ref, cbb_ref, n_lat)
    conv_silu(cl_ref, cl_s, cwc_ref, cbc_ref, n_lat)
    st_s[...] = jnp.zeros_like(st_s)
    forward(dtc_ref, xc_s, bc_s, cc_s, oc_s, n_ctx, need_ctx)
    backward(xc_s, bc_s, cc_s, zc_ref, yc_ref, oc_s, n_ctx, need_ctx)
    forward(dtl_ref, xl_s, bl_s, cl_s, ol_s, n_lat, True)
    backward(xl_s, bl_s, cl_s, zl_ref, yl_ref, ol_s, n_lat, True)


def _m2_group_lanes(v, groups):
    lead = v.shape[:-1]
    hpg = v.shape[-1] // (2 * groups)
    v = v.reshape(lead + (2, groups, hpg))
    v = jnp.moveaxis(v, -3, -2).reshape(lead + (groups, 2 * hpg))
    pad = jnp.zeros(lead + (groups, LANES - 2 * hpg), v.dtype)
    return jnp.concatenate([v, pad], axis=-1).reshape(lead + (groups * LANES,))


def _mamba2_scan(p_c, dt_c, p_l, dt_l, conv_w, conv_b, dt_bias, a_neg, d_skip, gn, need_ctx):
    bt, length, _ = p_l.shape
    n_c = p_c.shape[1]
    dinner = gn.shape[0]
    heads = dinner // M2_HEADDIM
    g = M2_GROUPS
    hpg = heads // g
    gw = dinner // g
    ns = M2_DSTATE
    conv_dim = conv_w.shape[1]
    zb = dinner // gw
    xb = 2 * dinner // ns

    bias_sel = _m2_group_lanes(dt_bias.astype(F32).reshape(1, -1), g).reshape(g, 1, LANES)
    aneg_sel = _m2_group_lanes(a_neg.astype(F32).reshape(1, -1), g).reshape(g, 1, LANES)
    head_of_lane = jnp.arange(gw) // M2_HEADDIM

    def spread(first_lane):
        return (jnp.arange(LANES)[:, None] == first_lane + head_of_lane[None, :]).astype(BF16)

    nd = 2 * hpg
    l_mid, l_lo, l_e, l_w, l_one = nd, 2 * nd, 4 * nd, 5 * nd, 6 * nd
    ex = jnp.concatenate([spread(l_e), spread(l_w), spread(l_e + hpg), spread(l_w + hpg)], axis=1)
    ede = jnp.concatenate([spread(0), spread(hpg)], axis=1)
    col_head = jnp.arange(nd * M2_CHUNK) // M2_CHUNK
    rows_idx = jnp.arange(LANES)[:, None]
    diff_const = ((rows_idx < l_lo + nd) & (rows_idx % nd == col_head[None, :])).astype(BF16)
    d_exp = jnp.repeat(d_skip.astype(F32), M2_HEADDIM).reshape(1, dinner)

    def cols(rows, width, off):
        return pl.BlockSpec((1, rows, width), lambda b, j: (b, 0, off + j))

    def specs(rows):
        return [cols(rows, gw, 0), cols(rows, gw, zb), cols(rows, ns, xb), cols(rows, ns, xb + g),
                pl.BlockSpec((1, rows, LANES), lambda b, j: (b, 0, j), pipeline_mode=pl.Buffered(1))]

    def cvec(nrows, width, off):
        return pl.BlockSpec((nrows, width), lambda b, j: (0, off + j))

    per_group = pl.BlockSpec((1, 1, LANES), lambda b, j: (j, 0, 0))
    in_specs = (specs(n_c) + specs(length)
                + [cvec(3, gw, 0), cvec(3, ns, dinner // ns), cvec(3, ns, dinner // ns + g),
                   cvec(1, gw, 0), cvec(1, ns, dinner // ns), cvec(1, ns, dinner // ns + g),
                   per_group, per_group,
                   _resident(ex.shape), _resident(ede.shape), _resident(diff_const.shape),
                   cvec(1, gw, 0), cvec(1, gw, 0)])
    out_specs = [cols(length, gw, 0)]
    out_shape = [jax.ShapeDtypeStruct((bt, length, dinner), BF16)]
    if need_ctx:
        out_specs.insert(0, cols(n_c, gw, 0))
        out_shape.insert(0, jax.ShapeDtypeStruct((bt, n_c, dinner), BF16))
    scratch = [pltpu.VMEM((length, gw), BF16), pltpu.VMEM((length, ns), BF16), pltpu.VMEM((length, ns), BF16),
               pltpu.VMEM((n_c, gw), BF16), pltpu.VMEM((n_c, ns), BF16), pltpu.VMEM((n_c, ns), BF16),
               pltpu.VMEM((length, gw), F32), pltpu.VMEM((n_c, gw), F32),
               pltpu.VMEM((ns, 2 * gw), F32),
               pltpu.VMEM((length, LANES), BF16), pltpu.VMEM((length // M2_CHUNK, 1, 2 * gw), F32),
               pltpu.VMEM((2, M2_CHUNK, LANES), BF16), pltpu.VMEM((2, nd, M2_CHUNK), F32),
               pltpu.VMEM((2, nd, nd * M2_CHUNK), BF16), pltpu.VMEM((2, 1, 2 * gw), F32)]
    cb = conv_b.reshape(1, conv_dim)
    res = pl.pallas_call(
        functools.partial(_m2_kernel, need_ctx=need_ctx, chunk=M2_CHUNK),
        grid=(bt, g),
        in_specs=in_specs, out_specs=out_specs, out_shape=out_shape, scratch_shapes=scratch,
        compiler_params=_cparams("parallel", "parallel"),
        name="mamba2_scan",
    )(p_c, p_c, p_c, p_c, dt_c, p_l, p_l, p_l, p_l, dt_l,
      conv_w, conv_w, conv_w, cb, cb, cb, bias_sel, aneg_sel, ex, ede, diff_const, d_exp,
      gn.reshape(1, dinner))
    return (res[0], res[1]) if need_ctx else (None, res[0])


def kernel(x, c, ctx, c_ctx, ada_w, ada_b, norm_g, ret_w_in, ret_w_out, ret_decay, ret_gn, hg_w_in, hg_w_out, hg_lb, hg_gn, m2_w_in, m2_w_out, m2_conv_w, m2_conv_b, m2_dt_bias, m2_a_log, m2_d, m2_gn, ffn_w_up, ffn_conv_w, ffn_conv_b, ffn_w_down):
    bt, length, d = x.shape
    n_c = ctx.shape[1]
    depth = ada_w.shape[0]
    hidden = ffn_conv_w.shape[-1]
    tm = min(512, length)
    tm_c = n_c

    rows = -(-(bt + 1) // 8) * 8
    cc = jnp.concatenate([c, c_ctx[None, :], jnp.zeros((rows - bt - 1, d), F32)], axis=0)
    mod = _modulation(cc, ada_w, ada_b)

    lb_cum = jnp.cumsum(jax.nn.softmax(hg_lb.astype(F32), axis=0), axis=0)
    lb_all = lb_cum - lb_cum[0]

    h_l, h_c = x, ctx
    for i in range(depth):
        kind, j = i % N_MIXERS, i // N_MIXERS
        need_ctx = i < depth - 1
        mod_l = mod[i, :bt].reshape(bt, 6, 1, d)
        mod_c = mod[i, bt].reshape(1, 6, 1, d)
        sh1_l, sc1_l, g1_l, sh2_l, sc2_l, g2_l = (mod_l[:, k] for k in range(6))
        sh1_c, sc1_c, g1_c, sh2_c, sc2_c, g2_c = (mod_c[:, k] for k in range(6))

        if kind == 0:
            w_in = ret_w_in[j].astype(BF16)
            p_l = _project(h_l, norm_g[i, 0], sc1_l, sh1_l, w_in, tm=tm)
            p_c = _project(h_c, norm_g[i, 0], sc1_c, sh1_c, w_in, tm=tm_c)
            log_gamma = jax.nn.log_sigmoid(ret_decay[j].astype(F32))
            y_c, y_l = _retention_scan(p_c, p_l, log_gamma, ret_gn[j], need_ctx)
            w_out = ret_w_out[j].astype(BF16)
        elif kind == 1:
            wq, wff, wfb, wi, wg = jnp.split(hg_w_in[j], 5, axis=-1)
            w_main = jnp.concatenate([wq, wi, wg], axis=-1).astype(BF16)
            w_gate = jnp.concatenate([wff, wfb], axis=-1).astype(BF16)
            p_l, f_l = _project(h_l, norm_g[i, 0], sc1_l, sh1_l, w_main, w_gate, tm=tm)
            p_c, f_c = _project(h_c, norm_g[i, 0], sc1_c, sh1_c, w_main, w_gate, tm=tm_c)
            y_c, y_l = _hgrn2_scan(p_c, f_c, p_l, f_l, lb_all[i], hg_gn[j], need_ctx)
            w_out = hg_w_out[j].astype(BF16)
        else:
            dinner = m2_gn.shape[-1]
            n_main = dinner + m2_conv_w.shape[-1]
            w_main = m2_w_in[j][:, :n_main].astype(BF16)
            w_dt = _m2_group_lanes(m2_w_in[j][:, n_main:], M2_GROUPS).astype(BF16)
            p_l, dt_l = _project(h_l, norm_g[i, 0], sc1_l, sh1_l, w_main, w_dt, tm=tm)
            p_c, dt_c = _project(h_c, norm_g[i, 0], sc1_c, sh1_c, w_main, w_dt, tm=tm_c)
            a_neg = -jnp.exp(m2_a_log[j].astype(F32))
            y_c, y_l = _mamba2_scan(p_c, dt_c, p_l, dt_l, m2_conv_w[j], m2_conv_b[j], m2_dt_bias[j], a_neg,
                                    m2_d[j], m2_gn[j], need_ctx)
            w_out = m2_w_out[j].astype(BF16)

        wa = ffn_w_up[i][:, :hidden].astype(BF16)
        wv = ffn_w_up[i][:, hidden:].astype(BF16)
        wd = ffn_w_down[i].astype(BF16)
        h_l = _out_project(y_l, w_out, h_l, norm_g[i, 1], g1_l, tm=min(2 * tm, length))
        h_l = _conv_ffn(h_l, norm_g[i, 2], sc2_l, sh2_l, wa, wv, ffn_conv_w[i], ffn_conv_b[i], wd,
                        norm_g[i, 3], g2_l, tm=tm, seg=GRID_W)
        if need_ctx:
            h_c = _out_project(y_c, w_out, h_c, norm_g[i, 1], g1_c, tm=tm_c)
            h_c = _conv_ffn(h_c, norm_g[i, 2], sc2_c, sh2_c, wa, wv, ffn_conv_w[i], ffn_conv_b[i], wd,
                            norm_g[i, 3], g2_c, tm=tm_c, seg=n_c)
    return h_l
```
